```python
import math
import jax, jax.numpy as jnp
from jax import lax
import numpy as np

D_MODEL = 4096
BATCH = 4
SEQ = 2048
DEPTH = 4
DEC_BATCH = 128
DEC_SEQ = 4
PAST_LEN = 16384
PAGE_SIZE = 128

N_HEADS = 16
HEAD_DK = D_MODEL // N_HEADS
HEAD_DV = 2 * HEAD_DK
D_V = N_HEADS * HEAD_DV
CHUNK = 128
ROPE_BASE = 10000.0
POOL_WINDOWS = (2, 4, 8, 16)
N_POOL_GROUPS = len(POOL_WINDOWS)
POOL_GROUP = D_MODEL // N_POOL_GROUPS
POOL_BUF = max(POOL_WINDOWS) - 1
D_FF = 4 * D_MODEL
N_RET = (DEPTH + 1) // 2
N_POOL = DEPTH // 2
EPS = 1e-6

kernel_name = "retnet_poolformer_hybrid_step"


def rms_norm(x, g):
    x32 = x.astype(jnp.float32)
    y = x32 * lax.rsqrt(jnp.mean(x32 * x32, axis=-1, keepdims=True) + EPS)
    return (y * g.astype(jnp.float32)).astype(x.dtype)


def rotary(x, pos):
    half = x.shape[-1] // 2
    inv = ROPE_BASE ** (-jnp.arange(half, dtype=jnp.float32) / half)
    ang = pos.astype(jnp.float32)[:, None] * inv[None, :]
    cos = jnp.cos(ang)[None, :, None, :]
    sin = jnp.sin(ang)[None, :, None, :]
    x1, x2 = x[..., :half], x[..., half:]
    return jnp.concatenate([x1 * cos - x2 * sin, x2 * cos + x1 * sin], axis=-1)


def retention_log_decay():
    return jnp.log1p(-jnp.exp2(-5.0 - jnp.arange(N_HEADS, dtype=jnp.float32)))


def retention_chunk_scan(q, k, v, state0, chunk):
    B, T, H, dk = q.shape
    dv = v.shape[-1]
    n = T // chunk
    logg = retention_log_decay()
    idx = jnp.arange(chunk, dtype=jnp.float32)
    diff = idx[:, None] - idx[None, :]
    causal = diff >= 0
    dmask = jnp.where(causal[None], jnp.exp(jnp.where(causal, diff, 0.0)[None] * logg[:, None, None]), 0.0)
    xi = jnp.exp((idx + 1.0)[:, None] * logg[None, :])
    zeta = jnp.exp((chunk - 1.0 - idx)[:, None] * logg[None, :])
    g_chunk = jnp.exp(chunk * logg)

    def split(a):
        return a.reshape(B, n, chunk, H, a.shape[-1]).swapaxes(0, 1)

    def step(R, qkv):
        qc, kc, vc = qkv
        s = jnp.einsum('bnhd,bmhd->bhnm', qc, kc) * dmask[None]
        inner = jnp.einsum('bhnm,bmhe->bnhe', s, vc)
        cross = jnp.einsum('bnhd,bhde->bnhe', qc, R) * xi[None, :, :, None]
        R_new = g_chunk[None, :, None, None] * R + jnp.einsum('bmhd,bmhe->bhde', kc * zeta[None, :, :, None], vc)
        return R_new, inner + cross

    R, o = lax.scan(step, state0, (split(q), split(k), split(v)))
    o = o.swapaxes(0, 1).reshape(B, T, H, dv)
    return o, R


def retention_mixer(u, w_in, w_out, state0, pos0):
    B, T, _ = u.shape
    chunk = CHUNK if T % CHUNK == 0 else T
    proj = jnp.einsum('btd,de->bte', u, w_in)
    q, k, v, g = jnp.split(proj, [D_MODEL, 2 * D_MODEL, 2 * D_MODEL + D_V], axis=-1)
    pos = pos0 + jnp.arange(T)
    q = rotary(q.reshape(B, T, N_HEADS, HEAD_DK).astype(jnp.float32), pos)
    k = rotary(k.reshape(B, T, N_HEADS, HEAD_DK).astype(jnp.float32), pos) * (HEAD_DK ** -0.5)
    v = v.reshape(B, T, N_HEADS, HEAD_DV).astype(jnp.float32)
    o, R = retention_chunk_scan(q, k, v, state0.astype(jnp.float32), chunk)
    o = o * lax.rsqrt(jnp.mean(o * o, axis=-1, keepdims=True) + EPS)
    o = o.reshape(B, T, D_V).astype(u.dtype) * jax.nn.silu(g)
    y = jnp.einsum('bte,ed->btd', o, w_out)
    return y, R.astype(state0.dtype)


def pool_mixer(u, buf, pos0, w_pool, pool_scale):
    B, T, D = u.shape
    ext = jnp.concatenate([buf.astype(u.dtype), u], axis=1)
    cs = jnp.cumsum(ext.astype(jnp.float32), axis=1)
    cs = jnp.concatenate([jnp.zeros((B, 1, D), jnp.float32), cs], axis=1)
    end = cs[:, POOL_BUF + 1:]
    pos = pos0 + jnp.arange(T)
    outs = []
    for gi, w in enumerate(POOL_WINDOWS):
        c0, c1 = gi * POOL_GROUP, (gi + 1) * POOL_GROUP
        start = cs[:, POOL_BUF + 1 - w:POOL_BUF + 1 - w + T, c0:c1]
        cnt = jnp.minimum(w, pos + 1).astype(jnp.float32)[None, :, None]
        d = (end[..., c0:c1] - start) / cnt - u[..., c0:c1].astype(jnp.float32)
        outs.append(jnp.einsum('btc,ce->bte', d.astype(u.dtype), w_pool[gi]))
    y = jnp.concatenate(outs, axis=-1) * pool_scale
    new_buf = ext[:, -POOL_BUF:]
    return y, new_buf.astype(buf.dtype)


def sq_relu_mlp(x, w_up, w_down):
    h = jax.nn.relu(jnp.einsum('btd,df->btf', x, w_up))
    return jnp.einsum('btf,fd->btd', h * h, w_down)


def run_trunk(x, ret_states, pool_bufs, pos0, w_ret_in, w_ret_out, w_pool, pool_scale,
              w_up, w_down, g_mix_pre, g_mix_post, g_mlp_pre, g_mlp_post):
    new_ret, new_pool = [], []
    for i in range(DEPTH):
        j = i // 2
        u = rms_norm(x, g_mix_pre[i])
        if i % 2 == 0:
            m, st = retention_mixer(u, w_ret_in[j], w_ret_out[j], ret_states[j], pos0)
            new_ret.append(st)
        else:
            m, st = pool_mixer(u, pool_bufs[j], pos0, w_pool[j], pool_scale[j])
            new_pool.append(st)
        x = x + rms_norm(m, g_mix_post[i])
        h = sq_relu_mlp(rms_norm(x, g_mlp_pre[i]), w_up[i], w_down[i])
        x = x + rms_norm(h, g_mlp_post[i])
    return x, jnp.stack(new_ret, axis=0), jnp.stack(new_pool, axis=0)


def setup_inputs(seed: int = 0) -> dict:
    key = jax.random.key(seed)
    ks = jax.random.split(key, 14)
    f32 = jnp.float32
    nrm = jax.random.normal
    return {
        "x_prompt": nrm(ks[0], (BATCH, SEQ, D_MODEL), f32),
        "x_sample": nrm(ks[1], (DEC_BATCH, DEC_SEQ, D_MODEL), f32),
        "state_ret": 0.5 * nrm(ks[2], (N_RET, DEC_BATCH, N_HEADS, HEAD_DK, HEAD_DV), f32),
        "state_pool": nrm(ks[3], (N_POOL, DEC_BATCH, POOL_BUF, D_MODEL), f32),
        "w_ret_in": nrm(ks[4], (N_RET, D_MODEL, 2 * D_MODEL + 2 * D_V), f32) * D_MODEL ** -0.5,
        "w_ret_out": nrm(ks[5], (N_RET, D_V, D_MODEL), f32) * D_V ** -0.5,
        "w_pool": nrm(ks[6], (N_POOL, N_POOL_GROUPS, POOL_GROUP, POOL_GROUP), f32) * POOL_GROUP ** -0.5,
        "pool_scale": 1.0 + 0.1 * nrm(ks[7], (N_POOL, D_MODEL), f32),
        "w_up": nrm(ks[8], (DEPTH, D_MODEL, D_FF), f32) * D_MODEL ** -0.5,
        "w_down": nrm(ks[9], (DEPTH, D_FF, D_MODEL), f32) * D_FF ** -0.5,
        "g_mix_pre": 1.0 + 0.1 * nrm(ks[10], (DEPTH, D_MODEL), f32),
        "g_mix_post": 1.0 + 0.1 * nrm(ks[11], (DEPTH, D_MODEL), f32),
        "g_mlp_pre": 1.0 + 0.1 * nrm(ks[12], (DEPTH, D_MODEL), f32),
        "g_mlp_post": 1.0 + 0.1 * nrm(ks[13], (DEPTH, D_MODEL), f32),
    }


def reference(x_prompt, x_sample, state_ret, state_pool, w_ret_in, w_ret_out, w_pool, pool_scale,
              w_up, w_down, g_mix_pre, g_mix_post, g_mlp_pre, g_mlp_post):
    B = x_prompt.shape[0]
    zero_ret = jnp.zeros((B, N_HEADS, HEAD_DK, HEAD_DV), state_ret.dtype)
    zero_pool = jnp.zeros((B, POOL_BUF, D_MODEL), state_pool.dtype)
    y_prompt, ret_state_prompt, pool_buf_prompt = run_trunk(
        x_prompt, [zero_ret] * N_RET, [zero_pool] * N_POOL, 0,
        w_ret_in, w_ret_out, w_pool, pool_scale, w_up, w_down,
        g_mix_pre, g_mix_post, g_mlp_pre, g_mlp_post)
    y_sample, ret_state_sample, pool_buf_sample = run_trunk(
        x_sample, [state_ret[j] for j in range(N_RET)], [state_pool[j] for j in range(N_POOL)], PAST_LEN,
        w_ret_in, w_ret_out, w_pool, pool_scale, w_up, w_down,
        g_mix_pre, g_mix_post, g_mlp_pre, g_mlp_post)
    return (y_prompt, y_sample, ret_state_prompt, pool_buf_prompt, ret_state_sample, pool_buf_sample)
```

```python
import functools

import jax
import jax.numpy as jnp
from jax import lax
from jax.experimental import pallas as pl
from jax.experimental.pallas import tpu as pltpu

N_HEADS = 16
CHUNK = 128
ROPE_BASE = 10000.0
POOL_WINDOWS = (2, 4, 8, 16)
PAST_LEN = 16384
EPS = 1e-6

F32 = jnp.float32
BF16 = jnp.bfloat16

VMEM_BYTES_V7X = 64 * 1024 * 1024
VMEM_LIMIT_BYTES = VMEM_BYTES_V7X - 8 * 1024 * 1024
LANES = 128
SAMPLE_ROWS_PADDED = 8


def _pick_block(n, target, mult):
    best = None
    for d in range(mult, min(n, target) + 1, mult):
        if n % d == 0:
            best = d
    return best if best is not None else n


def _params(sem):
    return pltpu.CompilerParams(dimension_semantics=sem, vmem_limit_bytes=VMEM_LIMIT_BYTES)


def _rms(x, g):
    ms = jnp.mean(x * x, axis=-1, keepdims=True)
    return x * lax.rsqrt(ms + EPS) * g


def _prenorm_kernel(x_ref, g_ref, u_ref):
    u_ref[...] = _rms(x_ref[...], g_ref[...]).astype(u_ref.dtype)


def _resid_norm_kernel(x_ref, m_ref, gpost_ref, gpre_ref, xo_ref, u_ref):
    xn = x_ref[...] + _rms(m_ref[...], gpost_ref[...])
    xo_ref[...] = xn
    u_ref[...] = _rms(xn, gpre_ref[...]).astype(u_ref.dtype)


def _resid_kernel(x_ref, m_ref, gpost_ref, xo_ref):
    xo_ref[...] = x_ref[...] + _rms(m_ref[...], gpost_ref[...])


def _row_specs(M, D, n_rows_in, n_vec, n_out):
    br = _pick_block(M, 272, 16)
    row = pl.BlockSpec((br, D), lambda i: (i, 0))
    vec = pl.BlockSpec((1, D), lambda i: (0, 0))
    return (M // br,), [row] * n_rows_in + [vec] * n_vec, [row] * n_out


def prenorm(x, g, dtype):
    M, D = x.shape
    grid, in_specs, out_specs = _row_specs(M, D, 1, 1, 1)
    return pl.pallas_call(
        _prenorm_kernel, grid=grid, in_specs=in_specs, out_specs=out_specs[0],
        out_shape=jax.ShapeDtypeStruct((M, D), dtype), compiler_params=_params(("parallel",)),
        name="prenorm")(x, g.reshape(1, D))


def resid_norm(x, m, g_post, g_pre, dtype):
    M, D = x.shape
    grid, in_specs, out_specs = _row_specs(M, D, 2, 2, 2)
    return pl.pallas_call(
        _resid_norm_kernel, grid=grid, in_specs=in_specs, out_specs=out_specs,
        out_shape=(jax.ShapeDtypeStruct((M, D), F32), jax.ShapeDtypeStruct((M, D), dtype)),
        compiler_params=_params(("parallel",)), name="resid_norm")(
            x, m, g_post.reshape(1, D), g_pre.reshape(1, D))


def resid(x, m, g_post):
    M, D = x.shape
    grid, in_specs, out_specs = _row_specs(M, D, 2, 1, 1)
    return pl.pallas_call(
        _resid_kernel, grid=grid, in_specs=in_specs, out_specs=out_specs[0],
        out_shape=jax.ShapeDtypeStruct((M, D), F32), compiler_params=_params(("parallel",)),
        name="resid")(x, m, g_post.reshape(1, D))


def _epi_plain(acc, rows, o_ref):
    o_ref[rows, :] = acc.astype(o_ref.dtype)


def _epi_relu2(acc, rows, o_ref):
    r = jnp.maximum(acc, 0.0)
    o_ref[rows, :] = (r * r).astype(o_ref.dtype)


def _epi_silu(acc, rows, o_ref):
    o_ref[rows, :] = (acc * (1.0 / (1.0 + jnp.exp(-acc)))).astype(o_ref.dtype)


def _epi_rotary(acc, rows, o_ref, cos_ref, sin_ref, *, head_dk, scale):
    half = head_dk // 2
    cos = cos_ref[rows, :]
    sin = sin_ref[rows, :]
    for h in range(acc.shape[1] // head_dk):
        c0 = h * head_dk
        x1 = acc[:, c0:c0 + half]
        x2 = acc[:, c0 + half:c0 + head_dk]
        o_ref[rows, c0:c0 + half] = ((x1 * cos - x2 * sin) * scale).astype(o_ref.dtype)
        o_ref[rows, c0 + half:c0 + head_dk] = ((x2 * cos + x1 * sin) * scale).astype(o_ref.dtype)


def _mm_kernel(x_ref, w_ref, *rest, epilogue, n_extra, n_k, sub_rows):
    extra = rest[:n_extra]
    o_ref = rest[n_extra]
    bm = x_ref.shape[0]
    row_slices = [pl.ds(r, sub_rows) for r in range(0, bm, sub_rows)]
    if n_k == 1:
        for rows in row_slices:
            acc = jnp.dot(x_ref[rows, :], w_ref[...], preferred_element_type=F32)
            epilogue(acc, rows, o_ref, *extra)
        return
    acc_ref = rest[n_extra + 1]
    k = pl.program_id(2)
    for rows in row_slices:
        part = jnp.dot(x_ref[rows, :], w_ref[...], preferred_element_type=F32)

        @pl.when(k == 0)
        def _():
            acc_ref[rows, :] = part

        @pl.when(jnp.logical_and(k > 0, k < n_k - 1))
        def _():
            acc_ref[rows, :] += part

        @pl.when(k == n_k - 1)
        def _():
            epilogue(acc_ref[rows, :] + part, rows, o_ref, *extra)


def matmul(x, w, *, col0=0, n_cols=None, out_dtype=F32, epilogue=_epi_plain, extras=(), bk=None):
    M, K = x.shape
    N = n_cols if n_cols is not None else w.shape[1]
    bm = _pick_block(M, 1088, 16)
    bn = _pick_block(N, 1024, 256)
    bk = K if bk is None else min(bk, K)
    n_k = K // bk
    sub_rows = _pick_block(bm, 272, 16)
    cb = col0 // bn
    assert col0 % bn == 0 and K % bk == 0
    in_specs = [pl.BlockSpec((bm, bk), lambda m, n, k: (m, k)),
                pl.BlockSpec((bk, bn), lambda m, n, k: (k, n + cb))]
    in_specs += [pl.BlockSpec(bs, im) for _, bs, im in extras]
    kern = functools.partial(_mm_kernel, epilogue=epilogue, n_extra=len(extras), n_k=n_k,
                             sub_rows=sub_rows)
    return pl.pallas_call(
        kern, grid=(M // bm, N // bn, n_k), in_specs=in_specs,
        out_specs=pl.BlockSpec((bm, bn), lambda m, n, k: (m, n)),
        out_shape=jax.ShapeDtypeStruct((M, N), out_dtype),
        scratch_shapes=[pltpu.VMEM((bm, bn), F32)] if n_k > 1 else [],
        compiler_params=_params(("parallel", "parallel", "arbitrary")),
        name="matmul")(x, w, *[a for a, _, _ in extras])


def _retention_tables(chunk, rows):
    logg = jnp.log1p(-jnp.exp2(-5.0 - jnp.arange(N_HEADS, dtype=F32)))
    idx = jnp.arange(chunk, dtype=F32)
    diff = idx[:, None] - idx[None, :]
    causal = diff >= 0
    dmask = jnp.where(causal[None], jnp.exp(jnp.where(causal, diff, 0.0)[None] * logg[:, None, None]), 0.0)
    xi = jnp.exp((idx + 1.0)[None, :] * logg[:, None])
    zeta = jnp.exp((chunk - 1.0 - idx)[None, :] * logg[:, None])
    g_chunk = jnp.exp(chunk * logg)
    p = rows - chunk
    dmask = jnp.pad(dmask, ((0, 0), (0, p), (0, p)))
    xi = jnp.pad(xi, ((0, 0), (0, p)))[:, :, None]
    zeta = jnp.pad(zeta, ((0, 0), (0, p)))[:, :, None]
    return dmask, xi, zeta, g_chunk[:, None, None]


def _retention_chunk(q, kf, v, gs, R, dmask, xi, zeta, g_chunk):
    s = lax.dot_general(q, kf.astype(BF16), (((1,), (1,)), ((), ())), preferred_element_type=F32)
    s = s * dmask
    inner = jnp.dot(s.astype(BF16), v, preferred_element_type=F32)
    cross = jnp.dot(q, R.astype(BF16), preferred_element_type=F32) * xi
    o = inner + cross
    kz = (kf * zeta).astype(BF16)
    upd = lax.dot_general(kz, v, (((0,), (0,)), ((), ())), preferred_element_type=F32)
    r_new = g_chunk * R + upd
    on = o * lax.rsqrt(jnp.mean(o * o, axis=-1, keepdims=True) + EPS)
    return on * gs, r_new


def _ret_prompt_kernel(q_ref, k_ref, v_ref, gs_ref, dm_ref, xi_ref, zt_ref, gc_ref, o_ref, st_ref, r_ref):
    c = pl.program_id(2)

    @pl.when(c == 0)
    def _():
        r_ref[...] = jnp.zeros_like(r_ref)

    og, r_new = _retention_chunk(q_ref[...], k_ref[...], v_ref[...], gs_ref[...], r_ref[...],
                                 dm_ref[...], xi_ref[...], zt_ref[...], gc_ref[...])
    o_ref[...] = og.astype(o_ref.dtype)
    r_ref[...] = r_new

    @pl.when(c == pl.num_programs(2) - 1)
    def _():
        st_ref[...] = r_new


def retention_prompt(q, k, v, gs, B, S):
    D = q.shape[1]
    DV = v.shape[1]
    dk, dv = D // N_HEADS, DV // N_HEADS
    C = CHUNK if S % CHUNK == 0 else S
    NC = S // C
    dmask, xi, zeta, gch = _retention_tables(C, C)
    tok = lambda width: pl.BlockSpec((C, width), lambda b, h, c: (b * NC + c, h))
    tab = lambda shape: pl.BlockSpec((None,) + shape, lambda b, h, c: (h, 0, 0))
    return pl.pallas_call(
        _ret_prompt_kernel, grid=(B, N_HEADS, NC),
        in_specs=[tok(dk), tok(dk), tok(dv), tok(dv), tab((C, C)), tab((C, 1)), tab((C, 1)), tab((1, 1))],
        out_specs=[tok(dv), pl.BlockSpec((None, None, dk, dv), lambda b, h, c: (b, h, 0, 0))],
        out_shape=(jax.ShapeDtypeStruct((B * S, DV), BF16),
                   jax.ShapeDtypeStruct((B, N_HEADS, dk, dv), F32)),
        scratch_shapes=[pltpu.VMEM((dk, dv), F32)],
        compiler_params=_params(("parallel", "parallel", "arbitrary")),
        name="retention_prompt")(q, k, v, gs, dmask, xi, zeta, gch)


def _ret_sample_kernel(q_ref, k_ref, v_ref, gs_ref, st_ref, dm_ref, xi_ref, zt_ref, gc_ref, o_ref, so_ref):
    dk, dv = st_ref.shape[2], st_ref.shape[3]
    for h in range(N_HEADS):
        ck = pl.ds(h * dk, dk)
        cv = pl.ds(h * dv, dv)
        og, r_new = _retention_chunk(q_ref[0, :, ck].astype(BF16), k_ref[0, :, ck],
                                     v_ref[0, :, cv].astype(BF16), gs_ref[0, :, cv], st_ref[0, h],
                                     dm_ref[h], xi_ref[h], zt_ref[h], gc_ref[h])
        o_ref[0, :, cv] = og
        so_ref[0, h] = r_new


def retention_sample(q, k, v, gs, state, chunk):
    BS, TP, D = q.shape
    DV = v.shape[2]
    H, dk, dv = state.shape[1:]
    tok = lambda width: pl.BlockSpec((1, TP, width), lambda b: (b, 0, 0))
    st = pl.BlockSpec((1, H, dk, dv), lambda b: (b, 0, 0, 0))
    full = lambda shape: pl.BlockSpec(shape, lambda b: (0,) * len(shape))
    dmask, xi, zeta, gch = _retention_tables(chunk, TP)
    return pl.pallas_call(
        _ret_sample_kernel, grid=(BS,),
        in_specs=[tok(D), tok(D), tok(DV), tok(DV), st, full((H, TP, TP)), full((H, TP, 1)),
                  full((H, TP, 1)), full((H, 1, 1))],
        out_specs=[tok(DV), st],
        out_shape=(jax.ShapeDtypeStruct((BS, TP, DV), F32), jax.ShapeDtypeStruct(state.shape, F32)),
        compiler_params=_params(("parallel",)),
        name="retention_sample")(q, k, v, gs, state, dmask, xi, zeta, gch)


def _pool_prompt_kernel(u_ref, w_ref, sc_ref, o_ref, hist_ref, *, hist_rows):
    g = pl.program_id(0)
    t = pl.program_id(2)
    tb = u_ref.shape[0]

    @pl.when(t == 0)
    def _():
        hist_ref[pl.ds(0, hist_rows), :] = jnp.zeros((hist_rows, hist_ref.shape[1]), F32)

    u = u_ref[...]
    hist_ref[pl.ds(hist_rows, tb), :] = u
    row = (t * tb + lax.broadcasted_iota(jnp.int32, (tb, 1), 0) + 1).astype(F32)
    for gi, win in enumerate(POOL_WINDOWS):

        @pl.when(g == gi)
        def _(win=win):
            ws = u
            for j in range(1, win):
                ws = ws + hist_ref[pl.ds(hist_rows - j, tb), :]
            cnt = jnp.minimum(row, float(win))
            d = ws / cnt - u
            y = jnp.dot(d.astype(BF16), w_ref[...], preferred_element_type=F32)
            o_ref[...] = y * sc_ref[...]

    hist_ref[pl.ds(0, hist_rows), :] = hist_ref[pl.ds(tb, hist_rows), :]


def pool_prompt(u, w_pool, scale, B, S):
    D = u.shape[1]
    G = len(POOL_WINDOWS)
    pg = D // G
    tb = _pick_block(S, 512, 8)
    nt = S // tb
    hist_rows = 16
    assert hist_rows >= max(POOL_WINDOWS) - 1 and tb >= hist_rows
    kern = functools.partial(_pool_prompt_kernel, hist_rows=hist_rows)
    blk = pl.BlockSpec((tb, pg), lambda g, b, t: (b * nt + t, g))
    return pl.pallas_call(
        kern, grid=(G, B, nt),
        in_specs=[blk, pl.BlockSpec((None, pg, pg), lambda g, b, t: (g, 0, 0)),
                  pl.BlockSpec((1, pg), lambda g, b, t: (0, g))],
        out_specs=blk,
        out_shape=jax.ShapeDtypeStruct((B * S, D), F32),
        scratch_shapes=[pltpu.VMEM((hist_rows + tb, pg), F32)],
        compiler_params=_params(("arbitrary", "arbitrary", "arbitrary")),
        name="pool_prompt")(u, w_pool, scale.reshape(1, D))


def _pool_sample_kernel(buf_ref, u_ref, w_ref, sc_ref, o_ref, *, pos0):
    g = pl.program_id(0)
    nbuf = buf_ref.shape[0]
    T = u_ref.shape[0]
    ext = lambda j: buf_ref[j] if j < nbuf else u_ref[j - nbuf]
    for gi, win in enumerate(POOL_WINDOWS):

        @pl.when(g == gi)
        def _(win=win):
            for t in range(T):
                ws = ext(nbuf + t)
                for j in range(1, win):
                    ws = ws + ext(nbuf + t - j)
                cnt = float(min(win, pos0 + t + 1))
                d = ws / cnt - u_ref[t]
                y = jnp.dot(d.astype(BF16), w_ref[...], preferred_element_type=F32)
                o_ref[t] = y * sc_ref[...]


def pool_sample(buf_t, u_t, w_pool, scale, pos0):
    nbuf, BS, D = buf_t.shape
    T = u_t.shape[0]
    G = len(POOL_WINDOWS)
    pg = D // G
    assert nbuf >= max(POOL_WINDOWS) - 1
    kern = functools.partial(_pool_sample_kernel, pos0=pos0)
    return pl.pallas_call(
        kern, grid=(G,),
        in_specs=[pl.BlockSpec((nbuf, BS, pg), lambda g: (0, 0, g)),
                  pl.BlockSpec((T, BS, pg), lambda g: (0, 0, g)),
                  pl.BlockSpec((None, pg, pg), lambda g: (g, 0, 0)),
                  pl.BlockSpec((1, pg), lambda g: (0, g))],
        out_specs=pl.BlockSpec((T, BS, pg), lambda g: (0, 0, g)),
        out_shape=jax.ShapeDtypeStruct((T, BS, D), F32),
        compiler_params=_params(("parallel",)),
        name="pool_sample")(buf_t, u_t, w_pool, scale.reshape(1, D))


def _rotary_tables(pos, half):
    inv = ROPE_BASE ** (-jnp.arange(half, dtype=F32) / half)
    ang = pos.astype(F32)[:, None] * inv[None, :]
    return jnp.cos(ang), jnp.sin(ang)


def kernel(x_prompt, x_sample, state_ret, state_pool, w_ret_in, w_ret_out, w_pool, pool_scale,
           w_up, w_down, g_mix_pre, g_mix_post, g_mlp_pre, g_mlp_post):
    B, S, D = x_prompt.shape
    BS, TS, _ = x_sample.shape
    BP, MS = B * S, BS * TS
    M = BP + MS
    depth = w_up.shape[0]
    dk = D // N_HEADS
    DV = w_ret_out.shape[1]
    nbuf = state_pool.shape[2]
    assert dk // 2 == LANES and TS <= SAMPLE_ROWS_PADDED and TS <= nbuf

    x = jnp.concatenate([x_prompt.reshape(BP, D), x_sample.reshape(MS, D)], axis=0)
    pos = jnp.concatenate([jnp.tile(jnp.arange(S), B), jnp.tile(PAST_LEN + jnp.arange(TS), BS)])
    cos, sin = _rotary_tables(pos, dk // 2)
    bm = _pick_block(M, 1088, 16)
    rot_extras = tuple((t, (bm, dk // 2), lambda m, n, k: (m, 0)) for t in (cos, sin))
    sample_chunk = CHUNK if TS % CHUNK == 0 else TS
    assert sample_chunk == TS

    w_in_b, w_out_b = w_ret_in.astype(BF16), w_ret_out.astype(BF16)
    w_pool_b, w_up_b, w_down_b = w_pool.astype(BF16), w_up.astype(BF16), w_down.astype(BF16)

    def pad_sample(a):
        a = a[BP:].astype(F32).reshape(BS, TS, a.shape[1])
        return jnp.pad(a, ((0, 0), (0, SAMPLE_ROWS_PADDED - TS), (0, 0)))

    ret_p, ret_s, pool_p, pool_s = [], [], [], []
    u = prenorm(x, g_mix_pre[0], BF16)
    for i in range(depth):
        j = i // 2
        if i % 2 == 0:
            w = w_in_b[j]
            q = matmul(u, w, col0=0, n_cols=D, out_dtype=BF16, extras=rot_extras,
                       epilogue=functools.partial(_epi_rotary, head_dk=dk, scale=1.0))
            k = matmul(u, w, col0=D, n_cols=D, out_dtype=F32, extras=rot_extras,
                       epilogue=functools.partial(_epi_rotary, head_dk=dk, scale=dk ** -0.5))
            v = matmul(u, w, col0=2 * D, n_cols=DV, out_dtype=BF16)
            gs = matmul(u, w, col0=2 * D + DV, n_cols=DV, out_dtype=F32, epilogue=_epi_silu)
            o_p, st_p = retention_prompt(q, k, v, gs, B, S)
            o_s, st_s = retention_sample(pad_sample(q), pad_sample(k), pad_sample(v), pad_sample(gs),
                                         state_ret[j], sample_chunk)
            o = jnp.concatenate([o_p, o_s[:, :TS].reshape(MS, DV).astype(BF16)], axis=0)
            m = matmul(o, w_out_b[j], bk=4096)
            ret_p.append(st_p)
            ret_s.append(st_s)
        else:
            u_p = u[:BP].reshape(B, S, D)
            u_s = u[BP:].reshape(BS, TS, D)
            y_p = pool_prompt(u, w_pool_b[j], pool_scale[j], B, S)
            y_s = pool_sample(state_pool[j].transpose(1, 0, 2), u_s.transpose(1, 0, 2), w_pool_b[j],
                              pool_scale[j], PAST_LEN)
            m = jnp.concatenate([y_p, y_s.transpose(1, 0, 2).reshape(MS, D)], axis=0)
            pool_p.append(u_p[:, S - nbuf:])
            pool_s.append(jnp.concatenate([state_pool[j][:, TS:], u_s], axis=1))
        x, un = resid_norm(x, m, g_mix_post[i], g_mlp_pre[i], BF16)
        h = matmul(un, w_up_b[i], out_dtype=BF16, epilogue=_epi_relu2)
        hd = matmul(h, w_down_b[i], bk=4096)
        if i + 1 < depth:
            x, u = resid_norm(x, hd, g_mlp_post[i], g_mix_pre[i + 1], BF16 if (i + 1) % 2 == 0 else F32)
        else:
            x = resid(x, hd, g_mlp_post[i])

    return (x[:BP].reshape(B, S, D), x[BP:].reshape(BS, TS, D),
            jnp.stack(ret_p), jnp.stack(pool_p), jnp.stack(ret_s), jnp.stack(pool_s))
```

```python
import functools

import jax
import jax.numpy as jnp
from jax import lax
from jax.experimental import pallas as pl
from jax.experimental.pallas import tpu as pltpu

N_HEADS = 16
CHUNK = 128
ROPE_BASE = 10000.0
POOL_WINDOWS = (2, 4, 8, 16)
PAST_LEN = 16384
EPS = 1e-6

F32 = jnp.float32
BF16 = jnp.bfloat16

VMEM_BYTES_V7X = 64 * 1024 * 1024
VMEM_LIMIT_BYTES = VMEM_BYTES_V7X - 8 * 1024 * 1024
LANES = 128
SAMPLE_ROWS_PADDED = 8


def _pick_block(n, target, mult):
    best = None
    for d in range(mult, min(n, target) + 1, mult):
        if n % d == 0:
            best = d
    return best if best is not None else n


def _params(sem):
    return pltpu.CompilerParams(dimension_semantics=sem, vmem_limit_bytes=VMEM_LIMIT_BYTES)


def _rms(x, g):
    ms = jnp.mean(x * x, axis=-1, keepdims=True)
    return x * lax.rsqrt(ms + EPS) * g


def _prenorm_kernel(x_ref, g_ref, u_ref):
    u_ref[...] = _rms(x_ref[...], g_ref[...]).astype(u_ref.dtype)


def _resid_norm_kernel(x_ref, m_ref, gpost_ref, gpre_ref, xo_ref, u_ref):
    xn = x_ref[...] + _rms(m_ref[...], gpost_ref[...])
    xo_ref[...] = xn
    u_ref[...] = _rms(xn, gpre_ref[...]).astype(u_ref.dtype)


def _resid_kernel(x_ref, m_ref, gpost_ref, xo_ref):
    xo_ref[...] = x_ref[...] + _rms(m_ref[...], gpost_ref[...])


def _row_specs(M, D, n_rows_in, n_vec, n_out):
    br = _pick_block(M, 272, 16)
    row = pl.BlockSpec((br, D), lambda i: (i, 0))
    vec = pl.BlockSpec((1, D), lambda i: (0, 0))
    return (M // br,), [row] * n_rows_in + [vec] * n_vec, [row] * n_out


def prenorm(x, g, dtype):
    M, D = x.shape
    grid, in_specs, out_specs = _row_specs(M, D, 1, 1, 1)
    return pl.pallas_call(
        _prenorm_kernel, grid=grid, in_specs=in_specs, out_specs=out_specs[0],
        out_shape=jax.ShapeDtypeStruct((M, D), dtype), compiler_params=_params(("parallel",)),
        name="prenorm")(x, g.reshape(1, D))


def resid_norm(x, m, g_post, g_pre, dtype):
    M, D = x.shape
    grid, in_specs, out_specs = _row_specs(M, D, 2, 2, 2)
    return pl.pallas_call(
        _resid_norm_kernel, grid=grid, in_specs=in_specs, out_specs=out_specs,
        out_shape=(jax.ShapeDtypeStruct((M, D), F32), jax.ShapeDtypeStruct((M, D), dtype)),
        compiler_params=_params(("parallel",)), name="resid_norm")(
            x, m, g_post.reshape(1, D), g_pre.reshape(1, D))


def resid(x, m, g_post):
    M, D = x.shape
    grid, in_specs, out_specs = _row_specs(M, D, 2, 1, 1)
    return pl.pallas_call(
        _resid_kernel, grid=grid, in_specs=in_specs, out_specs=out_specs[0],
        out_shape=jax.ShapeDtypeStruct((M, D), F32), compiler_params=_params(("parallel",)),
        name="resid")(x, m, g_post.reshape(1, D))


def _epi_plain(acc, rows, o_ref):
    o_ref[rows, :] = acc.astype(o_ref.dtype)


def _epi_relu2(acc, rows, o_ref):
    r = jnp.maximum(acc, 0.0)
    o_ref[rows, :] = (r * r).astype(o_ref.dtype)


def _epi_silu(acc, rows, o_ref):
    o_ref[rows, :] = (acc * (1.0 / (1.0 + jnp.exp(-acc)))).astype(o_ref.dtype)


def _epi_rotary(acc, rows, o_ref, cos_ref, sin_ref, *, head_dk, scale):
    half = head_dk // 2
    cos = cos_ref[rows, :]
    sin = sin_ref[rows, :]
    for h in range(acc.shape[1] // head_dk):
        c0 = h * head_dk
        x1 = acc[:, c0:c0 + half]
        x2 = acc[:, c0 + half:c0 + head_dk]
        o_ref[rows, c0:c0 + half] = ((x1 * cos - x2 * sin) * scale).astype(o_ref.dtype)
        o_ref[rows, c0 + half:c0 + head_dk] = ((x2 * cos + x1 * sin) * scale).astype(o_ref.dtype)


def _mm_kernel(x_ref, w_ref, *rest, epilogue, n_extra, n_k, sub_rows, has_side):
    extra = rest[:n_extra]
    if has_side:
        side_in_ref, o_ref, side_out_ref = rest[n_extra:n_extra + 3]
        side_out_ref[...] = side_in_ref[...].astype(BF16)
    else:
        o_ref = rest[n_extra]
    if w_ref.dtype == BF16:
        weights = lambda: w_ref[...]
    else:
        w_bf16 = w_ref[...].astype(BF16)
        weights = lambda: w_bf16
    bm = x_ref.shape[0]
    row_slices = [pl.ds(r, sub_rows) for r in range(0, bm, sub_rows)]
    if n_k == 1:
        for rows in row_slices:
            acc = jnp.dot(x_ref[rows, :], weights(), preferred_element_type=F32)
            epilogue(acc, rows, o_ref, *extra)
        return

    @pl.when(pl.program_id(2) == 0)
    def _():
        o_ref[...] = jnp.zeros_like(o_ref)

    for rows in row_slices:
        o_ref[rows, :] += jnp.dot(x_ref[rows, :], weights(), preferred_element_type=F32)


def matmul(x, w, layer=None, *, col0=0, n_cols=None, out_dtype=F32, epilogue=_epi_plain, extras=(),
           bk=None, side=None):
    M, K = x.shape
    N = n_cols if n_cols is not None else w.shape[-1]
    bm = _pick_block(M, 1088, 16)
    bn = _pick_block(N, 1024 if w.dtype == BF16 else 512, 256)
    bk = K if bk is None else min(bk, K)
    n_k = K // bk
    sub_rows = _pick_block(bm, 272, 16)
    cb = col0 // bn
    grid = (M // bm, N // bn, n_k)
    assert col0 % bn == 0 and K % bk == 0
    assert n_k == 1 or (epilogue is _epi_plain and out_dtype == F32)
    if w.ndim == 2:
        w_spec = pl.BlockSpec((bk, bn), lambda m, n, k: (k, n + cb))
    else:
        w_spec = pl.BlockSpec((None, bk, bn), lambda m, n, k: (layer, k, n + cb))
    args = [x, w] + [a for a, _, _ in extras]
    in_specs = [pl.BlockSpec((bm, bk), lambda m, n, k: (m, k)), w_spec]
    in_specs += [pl.BlockSpec(bs, im) for _, bs, im in extras]
    out_specs = [pl.BlockSpec((bm, bn), lambda m, n, k: (m, n))]
    out_shape = [jax.ShapeDtypeStruct((M, N), out_dtype)]
    if side is not None:
        src, sl = side
        n_steps = grid[0] * grid[1] * grid[2]
        R, C = src.shape[1:]
        rows = R // n_steps
        assert R % n_steps == 0 and rows % 16 == 0
        step = lambda m, n, k: (m * grid[1] + n) * grid[2] + k
        args.append(src)
        in_specs.append(pl.BlockSpec((None, rows, C), lambda m, n, k: (sl, step(m, n, k), 0)))
        out_specs.append(pl.BlockSpec((rows, C), lambda m, n, k: (step(m, n, k), 0)))
        out_shape.append(jax.ShapeDtypeStruct((R, C), BF16))
    kern = functools.partial(_mm_kernel, epilogue=epilogue, n_extra=len(extras), n_k=n_k,
                             sub_rows=sub_rows, has_side=side is not None)
    out = pl.pallas_call(
        kern, grid=grid, in_specs=in_specs, out_specs=out_specs, out_shape=out_shape,
        compiler_params=_params(("parallel", "parallel", "arbitrary")),
        name="matmul")(*args)
    return out if side is not None else out[0]


def _retention_tables(chunk, rows):
    logg = jnp.log1p(-jnp.exp2(-5.0 - jnp.arange(N_HEADS, dtype=F32)))
    idx = jnp.arange(chunk, dtype=F32)
    diff = idx[:, None] - idx[None, :]
    causal = diff >= 0
    dmask = jnp.where(causal[None], jnp.exp(jnp.where(causal, diff, 0.0)[None] * logg[:, None, None]), 0.0)
    xi = jnp.exp((idx + 1.0)[None, :] * logg[:, None])
    zeta = jnp.exp((chunk - 1.0 - idx)[None, :] * logg[:, None])
    g_chunk = jnp.exp(chunk * logg)
    p = rows - chunk
    dmask = jnp.pad(dmask, ((0, 0), (0, p), (0, p)))
    xi = jnp.pad(xi, ((0, 0), (0, p)))[:, :, None]
    zeta = jnp.pad(zeta, ((0, 0), (0, p)))[:, :, None]
    return dmask, xi, zeta, g_chunk[:, None, None]


def _retention_chunk(q, kf, v, gs, R, dmask, xi, zeta, g_chunk):
    s = lax.dot_general(q, kf.astype(BF16), (((1,), (1,)), ((), ())), preferred_element_type=F32)
    s = s * dmask
    inner = jnp.dot(s.astype(BF16), v, preferred_element_type=F32)
    cross = jnp.dot(q, R.astype(BF16), preferred_element_type=F32) * xi
    o = inner + cross
    kz = (kf * zeta).astype(BF16)
    upd = lax.dot_general(kz, v, (((0,), (0,)), ((), ())), preferred_element_type=F32)
    r_new = g_chunk * R + upd
    on = o * lax.rsqrt(jnp.mean(o * o, axis=-1, keepdims=True) + EPS)
    return on * gs, r_new


def _alias_previous(prev, args, in_specs, out_index):
    if prev is None:
        return {}
    args.append(prev)
    in_specs.append(pl.BlockSpec(memory_space=pl.ANY))
    return {len(args) - 1: out_index}


def _ret_prompt_kernel(q_ref, k_ref, v_ref, gs_ref, dm_ref, xi_ref, zt_ref, gc_ref, *rest, chunk):
    o_ref, st_ref = rest[-2:]
    st_ref[...] = jnp.zeros_like(st_ref)
    for c in range(q_ref.shape[0] // chunk):
        rows = pl.ds(c * chunk, chunk)
        og, r_new = _retention_chunk(q_ref[rows, :], k_ref[rows, :], v_ref[rows, :], gs_ref[rows, :],
                                     st_ref[...], dm_ref[...], xi_ref[...], zt_ref[...], gc_ref[...])
        o_ref[rows, :] = og.astype(o_ref.dtype)
        st_ref[...] = r_new


def retention_prompt(q, k, v, gs, B, S, layer, n_layers, prev_states):
    M, D = q.shape
    DV = v.shape[1]
    dk, dv = D // N_HEADS, DV // N_HEADS
    C = CHUNK if S % CHUNK == 0 else S
    dmask, xi, zeta, gch = _retention_tables(C, C)
    tok = lambda width: pl.BlockSpec((S, width), lambda b, h: (b, h))
    tab = lambda shape: pl.BlockSpec((None,) + shape, lambda b, h: (h, 0, 0))
    args = [q, k, v, gs, dmask, xi, zeta, gch]
    in_specs = [tok(dk), tok(dk), tok(dv), tok(dv), tab((C, C)), tab((C, 1)), tab((C, 1)), tab((1, 1))]
    aliases = _alias_previous(prev_states, args, in_specs, 1)
    return pl.pallas_call(
        functools.partial(_ret_prompt_kernel, chunk=C), grid=(B, N_HEADS),
        in_specs=in_specs,
        out_specs=[tok(dv), pl.BlockSpec((None, None, None, dk, dv), lambda b, h: (layer, b, h, 0, 0))],
        out_shape=(jax.ShapeDtypeStruct((M, DV), BF16),
                   jax.ShapeDtypeStruct((n_layers, B, N_HEADS, dk, dv), F32)),
        input_output_aliases=aliases,
        compiler_params=_params(("parallel", "parallel")),
        name="retention_prompt")(*args)


def _ret_sample_kernel(q_ref, k_ref, v_ref, gs_ref, st_ref, dm_ref, xi_ref, zt_ref, gc_ref, *rest):
    o_ref, so_ref = rest[-2:]
    dk, dv = st_ref.shape[2], st_ref.shape[3]
    for h in range(N_HEADS):
        ck = pl.ds(h * dk, dk)
        cv = pl.ds(h * dv, dv)
        og, r_new = _retention_chunk(q_ref[0, :, ck].astype(BF16), k_ref[0, :, ck],
                                     v_ref[0, :, cv].astype(BF16), gs_ref[0, :, cv], st_ref[0, h],
                                     dm_ref[h], xi_ref[h], zt_ref[h], gc_ref[h])
        o_ref[0, :, cv] = og
        so_ref[0, h] = r_new


def retention_sample(q, k, v, gs, states, layer, prev_states, chunk):
    BS, TP, D = q.shape
    DV = v.shape[2]
    H, dk, dv = states.shape[2:]
    tok = lambda width: pl.BlockSpec((1, TP, width), lambda b: (b, 0, 0))
    st = pl.BlockSpec((None, 1, H, dk, dv), lambda b: (layer, b, 0, 0, 0))
    full = lambda shape: pl.BlockSpec(shape, lambda b: (0,) * len(shape))
    dmask, xi, zeta, gch = _retention_tables(chunk, TP)
    args = [q, k, v, gs, states, dmask, xi, zeta, gch]
    in_specs = [tok(D), tok(D), tok(DV), tok(DV), st, full((H, TP, TP)), full((H, TP, 1)),
                full((H, TP, 1)), full((H, 1, 1))]
    aliases = _alias_previous(prev_states, args, in_specs, 1)
    return pl.pallas_call(
        _ret_sample_kernel, grid=(BS,),
        in_specs=in_specs,
        out_specs=[tok(DV), st],
        out_shape=(jax.ShapeDtypeStruct((BS, TP, DV), F32), jax.ShapeDtypeStruct(states.shape, F32)),
        input_output_aliases=aliases,
        compiler_params=_params(("parallel",)),
        name="retention_sample")(*args)


def _pool_prompt_kernel(u_ref, w_ref, sc_ref, o_ref, hist_ref, *, hist_rows):
    g = pl.program_id(0)
    t = pl.program_id(2)
    tb = u_ref.shape[0]

    @pl.when(t == 0)
    def _():
        hist_ref[pl.ds(0, hist_rows), :] = jnp.zeros((hist_rows, hist_ref.shape[1]), F32)

    u = u_ref[...]
    hist_ref[pl.ds(hist_rows, tb), :] = u
    row = (t * tb + lax.broadcasted_iota(jnp.int32, (tb, 1), 0) + 1).astype(F32)
    for gi, win in enumerate(POOL_WINDOWS):

        @pl.when(g == gi)
        def _(win=win):
            ws = u
            for j in range(1, win):
                ws = ws + hist_ref[pl.ds(hist_rows - j, tb), :]
            cnt = jnp.minimum(row, float(win))
            d = ws / cnt - u
            y = jnp.dot(d.astype(BF16), w_ref[...].astype(BF16), preferred_element_type=F32)
            o_ref[...] = y * sc_ref[...]

    hist_ref[pl.ds(0, hist_rows), :] = hist_ref[pl.ds(tb, hist_rows), :]


def pool_prompt(u, w_pool, scale, layer, B, S):
    M, D = u.shape
    G = len(POOL_WINDOWS)
    pg = D // G
    tb = _pick_block(S, 512, 8)
    nt = S // tb
    hist_rows = 16
    assert hist_rows >= max(POOL_WINDOWS) - 1 and tb >= hist_rows
    kern = functools.partial(_pool_prompt_kernel, hist_rows=hist_rows)
    blk = pl.BlockSpec((tb, pg), lambda g, b, t: (b * nt + t, g))
    return pl.pallas_call(
        kern, grid=(G, B, nt),
        in_specs=[blk, pl.BlockSpec((None, None, pg, pg), lambda g, b, t: (layer, g, 0, 0)),
                  pl.BlockSpec((None, 1, pg), lambda g, b, t: (layer, 0, g))],
        out_specs=blk,
        out_shape=jax.ShapeDtypeStruct((M, D), F32),
        scratch_shapes=[pltpu.VMEM((hist_rows + tb, pg), F32)],
        compiler_params=_params(("arbitrary", "arbitrary", "arbitrary")),
        name="pool_prompt")(u, w_pool, scale)


def _pool_sample_kernel(buf_ref, u_ref, w_ref, sc_ref, o_ref, *, pos0):
    g = pl.program_id(0)
    nbuf = buf_ref.shape[0]
    T = u_ref.shape[0]
    ext = lambda j: buf_ref[j] if j < nbuf else u_ref[j - nbuf]
    for gi, win in enumerate(POOL_WINDOWS):

        @pl.when(g == gi)
        def _(win=win):
            for t in range(T):
                ws = ext(nbuf + t)
                for j in range(1, win):
                    ws = ws + ext(nbuf + t - j)
                cnt = float(min(win, pos0 + t + 1))
                d = ws / cnt - u_ref[t]
                y = jnp.dot(d.astype(BF16), w_ref[...].astype(BF16), preferred_element_type=F32)
                o_ref[t] = y * sc_ref[...]


def pool_sample(buf_t, u_t, w_pool, scale, layer, pos0):
    nbuf, BS, D = buf_t.shape
    T = u_t.shape[0]
    G = len(POOL_WINDOWS)
    pg = D // G
    assert nbuf >= max(POOL_WINDOWS) - 1
    kern = functools.partial(_pool_sample_kernel, pos0=pos0)
    return pl.pallas_call(
        kern, grid=(G,),
        in_specs=[pl.BlockSpec((nbuf, BS, pg), lambda g: (0, 0, g)),
                  pl.BlockSpec((T, BS, pg), lambda g: (0, 0, g)),
                  pl.BlockSpec((None, None, pg, pg), lambda g: (layer, g, 0, 0)),
                  pl.BlockSpec((None, 1, pg), lambda g: (layer, 0, g))],
        out_specs=pl.BlockSpec((T, BS, pg), lambda g: (0, 0, g)),
        out_shape=jax.ShapeDtypeStruct((T, BS, D), F32),
        compiler_params=_params(("parallel",)),
        name="pool_sample")(buf_t, u_t, w_pool, scale)


def _rotary_tables(pos, half):
    inv = ROPE_BASE ** (-jnp.arange(half, dtype=F32) / half)
    ang = pos.astype(F32)[:, None] * inv[None, :]
    return jnp.cos(ang), jnp.sin(ang)


def kernel(x_prompt, x_sample, state_ret, state_pool, w_ret_in, w_ret_out, w_pool, pool_scale,
           w_up, w_down, g_mix_pre, g_mix_post, g_mlp_pre, g_mlp_post):
    B, S, D = x_prompt.shape
    BS, TS, _ = x_sample.shape
    BP, MS = B * S, BS * TS
    M = BP + MS
    depth = w_up.shape[0]
    dk = D // N_HEADS
    DV = w_ret_out.shape[1]
    nbuf = state_pool.shape[2]
    assert dk // 2 == LANES and TS <= SAMPLE_ROWS_PADDED and TS <= nbuf

    x = jnp.concatenate([x_prompt.reshape(BP, D), x_sample.reshape(MS, D)], axis=0)
    pos = jnp.concatenate([jnp.tile(jnp.arange(S), B), jnp.tile(PAST_LEN + jnp.arange(TS), BS)])
    cos, sin = _rotary_tables(pos, dk // 2)
    bm = _pick_block(M, 1088, 16)
    rot_extras = tuple((t, (bm, dk // 2), lambda m, n, k: (m, 0)) for t in (cos, sin))
    sample_chunk = CHUNK if TS % CHUNK == 0 else TS
    assert sample_chunk == TS

    def pad_sample(a):
        a = a[BP:].astype(F32).reshape(BS, TS, a.shape[1])
        return jnp.pad(a, ((0, 0), (0, SAMPLE_ROWS_PADDED - TS), (0, 0)))

    n_ret = state_ret.shape[0]
    scale3 = pool_scale.reshape(pool_scale.shape[0], 1, D)
    ret_p = ret_s = None
    pool_p, pool_s = [], []
    u = prenorm(x, g_mix_pre[0], BF16)
    for i in range(depth):
        j = i // 2
        if i % 2 == 0:
            q = matmul(u, w_ret_in, j, col0=0, n_cols=D, out_dtype=BF16, extras=rot_extras,
                       epilogue=functools.partial(_epi_rotary, head_dk=dk, scale=1.0))
            k = matmul(u, w_ret_in, j, col0=D, n_cols=D, out_dtype=F32, extras=rot_extras,
                       epilogue=functools.partial(_epi_rotary, head_dk=dk, scale=dk ** -0.5))
            v, w_out_b = matmul(u, w_ret_in, j, col0=2 * D, n_cols=DV, out_dtype=BF16, side=(w_ret_out, j))
            gs = matmul(u, w_ret_in, j, col0=2 * D + DV, n_cols=DV, out_dtype=F32, epilogue=_epi_silu)
            o, ret_p = retention_prompt(q, k, v, gs, B, S, j, n_ret, ret_p)
            o_s, ret_s = retention_sample(pad_sample(q), pad_sample(k), pad_sample(v), pad_sample(gs),
                                          state_ret, j, ret_s, sample_chunk)
            o = lax.dynamic_update_slice(o, o_s[:, :TS].reshape(MS, DV).astype(BF16), (BP, 0))
            m = matmul(o, w_out_b, bk=4096)
        else:
            u_s = u[BP:].reshape(BS, TS, D)
            m = pool_prompt(u, w_pool, scale3, j, B, S)
            y_s = pool_sample(state_pool[j].transpose(1, 0, 2), u_s.transpose(1, 0, 2), w_pool,
                              scale3, j, PAST_LEN)
            m = lax.dynamic_update_slice(m, y_s.transpose(1, 0, 2).reshape(MS, D), (BP, 0))
            pool_p.append(jnp.stack([u[(b + 1) * S - nbuf:(b + 1) * S] for b in range(B)]))
            pool_s.append(jnp.concatenate([state_pool[j][:, TS:], u_s], axis=1))
        x, un = resid_norm(x, m, g_mix_post[i], g_mlp_pre[i], BF16)
        h, w_down_b = matmul(un, w_up, i, out_dtype=BF16, epilogue=_epi_relu2, side=(w_down, i))
        hd = matmul(h, w_down_b, bk=4096)
        if i + 1 < depth:
            x, u = resid_norm(x, hd, g_mlp_post[i], g_mix_pre[i + 1], BF16 if (i + 1) % 2 == 0 else F32)
        else:
            x = resid(x, hd, g_mlp_post[i])

    return (x[:BP].reshape(B, S, D), x[BP:].reshape(BS, TS, D),
            ret_p, jnp.stack(pool_p), ret_s, jnp.stack(pool_s))
```

```python
import functools
import math

import jax
import jax.numpy as jnp
from jax import lax
from jax.experimental import pallas as pl
from jax.experimental.pallas import tpu as pltpu

N_HEADS = 16
CHUNK = 128
ROPE_BASE = 10000.0
POOL_WINDOWS = (2, 4, 8, 16)
PAST_LEN = 16384
EPS = 1e-6

F32 = jnp.float32
BF16 = jnp.bfloat16

VMEM_BYTES_V7X = 64 * 1024 * 1024
VMEM_LIMIT_BYTES = VMEM_BYTES_V7X - 8 * 1024 * 1024
LANES = 128
BF16_TILE_ROWS = 16


def _pick_block(n, target, mult):
    best = None
    for d in range(mult, min(n, target) + 1, mult):
        if n % d == 0:
            best = d
    return best if best is not None else n


def _params(sem):
    return pltpu.CompilerParams(dimension_semantics=sem, vmem_limit_bytes=VMEM_LIMIT_BYTES)


def _rms(x, g):
    ms = jnp.mean(x * x, axis=-1, keepdims=True)
    return x * lax.rsqrt(ms + EPS) * g


def _stacked_block(xp_ref, xs_ref, nbp):
    return jnp.where(pl.program_id(0) < nbp, xp_ref[...], xs_ref[...])


def _prenorm_kernel(xp_ref, xs_ref, g_ref, u_ref, *, nbp):
    u_ref[...] = _rms(_stacked_block(xp_ref, xs_ref, nbp), g_ref[...]).astype(u_ref.dtype)


def _resid_norm_kernel(*refs, nbp):
    *x_refs, m_ref, gpost_ref, gpre_ref, xo_ref, u_ref = refs
    x = x_refs[0][...] if len(x_refs) == 1 else _stacked_block(*x_refs, nbp)
    xn = x + _rms(m_ref[...], gpost_ref[...])
    xo_ref[...] = xn
    u_ref[...] = _rms(xn, gpre_ref[...]).astype(u_ref.dtype)


def _resid_split_kernel(x_ref, m_ref, gpost_ref, yp_ref, ys_ref, *, nbp):
    y = x_ref[...] + _rms(m_ref[...], gpost_ref[...])
    i = pl.program_id(0)

    @pl.when(i < nbp)
    def _():
        yp_ref[...] = y

    @pl.when(i >= nbp)
    def _():
        ys_ref[...] = y


def _row_specs(BP, MS, D):
    br = _pick_block(math.gcd(BP, MS), 256, 16)
    nbp, nbs = BP // br, MS // br
    whole = pl.BlockSpec((br, D), lambda i: (i, 0))
    prompt = pl.BlockSpec((br, D), lambda i: (jnp.minimum(i, nbp - 1), 0))
    sample = pl.BlockSpec((br, D), lambda i: (jnp.maximum(i - nbp, 0), 0))
    vec = pl.BlockSpec((1, D), lambda i: (0, 0))
    return nbp, (nbp + nbs,), whole, prompt, sample, vec


def prenorm(xp, xs, g, dtype):
    (BP, D), MS = xp.shape, xs.shape[0]
    nbp, grid, whole, prompt, sample, vec = _row_specs(BP, MS, D)
    return pl.pallas_call(
        functools.partial(_prenorm_kernel, nbp=nbp), grid=grid,
        in_specs=[prompt, sample, vec], out_specs=whole,
        out_shape=jax.ShapeDtypeStruct((BP + MS, D), dtype), compiler_params=_params(("parallel",)),
        name="prenorm")(xp, xs, g.reshape(1, D))


def resid_norm(x, m, g_post, g_pre, dtype, BP):
    M, D = m.shape
    nbp, grid, whole, prompt, sample, vec = _row_specs(BP, M - BP, D)
    xs, x_specs = ((x,), [whole]) if not isinstance(x, tuple) else (x, [prompt, sample])
    return pl.pallas_call(
        functools.partial(_resid_norm_kernel, nbp=nbp), grid=grid,
        in_specs=x_specs + [whole, vec, vec], out_specs=[whole, whole],
        out_shape=(jax.ShapeDtypeStruct((M, D), F32), jax.ShapeDtypeStruct((M, D), dtype)),
        compiler_params=_params(("parallel",)), name="resid_norm")(
            *xs, m, g_post.reshape(1, D), g_pre.reshape(1, D))


def resid_split(x, m, g_post, BP):
    M, D = x.shape
    nbp, grid, whole, prompt, sample, vec = _row_specs(BP, M - BP, D)
    return pl.pallas_call(
        functools.partial(_resid_split_kernel, nbp=nbp), grid=grid,
        in_specs=[whole, whole, vec], out_specs=[prompt, sample],
        out_shape=(jax.ShapeDtypeStruct((BP, D), F32), jax.ShapeDtypeStruct((M - BP, D), F32)),
        compiler_params=_params(("arbitrary",)), name="resid_split")(x, m, g_post.reshape(1, D))


def _epi_plain(acc, rows, o_ref):
    o_ref[rows, :] = acc.astype(o_ref.dtype)


def _epi_relu2(acc, rows, o_ref):
    r = jnp.maximum(acc, 0.0)
    o_ref[rows, :] = (r * r).astype(o_ref.dtype)


def _epi_silu(acc, rows, o_ref):
    o_ref[rows, :] = (acc * (1.0 / (1.0 + jnp.exp(-acc)))).astype(o_ref.dtype)


def _epi_rotary(acc, rows, o_ref, cos_ref, sin_ref, *, head_dk, scale):
    half = head_dk // 2
    cos = cos_ref[rows, :]
    sin = sin_ref[rows, :]
    for h in range(acc.shape[1] // head_dk):
        c0 = h * head_dk
        x1 = acc[:, c0:c0 + half]
        x2 = acc[:, c0 + half:c0 + head_dk]
        o_ref[rows, c0:c0 + half] = ((x1 * cos - x2 * sin) * scale).astype(o_ref.dtype)
        o_ref[rows, c0 + half:c0 + head_dk] = ((x2 * cos + x1 * sin) * scale).astype(o_ref.dtype)


def _mm_kernel(x_ref, w_ref, *rest, epilogue, n_extra, n_k, sub_rows, has_side):
    extra = rest[:n_extra]
    if has_side:
        side_in_ref, o_ref, side_out_ref = rest[n_extra:n_extra + 3]
        side_out_ref[...] = side_in_ref[...].astype(BF16)
    else:
        o_ref = rest[n_extra]
    if w_ref.dtype == BF16:
        weights = lambda: w_ref[...]
    else:
        w_bf16 = w_ref[...].astype(BF16)
        weights = lambda: w_bf16
    bm = x_ref.shape[0]
    row_slices = [pl.ds(r, sub_rows) for r in range(0, bm, sub_rows)]
    if n_k == 1:
        for rows in row_slices:
            acc = jnp.dot(x_ref[rows, :], weights(), preferred_element_type=F32)
            epilogue(acc, rows, o_ref, *extra)
        return

    @pl.when(pl.program_id(2) == 0)
    def _():
        o_ref[...] = jnp.zeros_like(o_ref)

    for rows in row_slices:
        o_ref[rows, :] += jnp.dot(x_ref[rows, :], weights(), preferred_element_type=F32)


def matmul(x, w, layer=None, *, col0=0, n_cols=None, out_dtype=F32, epilogue=_epi_plain, extras=(),
           bk=None, side=None):
    M, K = x.shape
    N = n_cols if n_cols is not None else w.shape[-1]
    bm = _pick_block(M, 1088, 16)
    bn = _pick_block(N, 1024 if w.dtype == BF16 else 512, 256)
    bk = K if bk is None else min(bk, K)
    n_k = K // bk
    sub_rows = _pick_block(bm, 272, 16)
    cb = col0 // bn
    grid = (M // bm, N // bn, n_k)
    assert col0 % bn == 0 and K % bk == 0
    assert n_k == 1 or (epilogue is _epi_plain and out_dtype == F32)
    if w.ndim == 2:
        w_spec = pl.BlockSpec((bk, bn), lambda m, n, k: (k, n + cb))
    else:
        w_spec = pl.BlockSpec((None, bk, bn), lambda m, n, k: (layer, k, n + cb))
    args = [x, w] + [a for a, _, _ in extras]
    in_specs = [pl.BlockSpec((bm, bk), lambda m, n, k: (m, k)), w_spec]
    in_specs += [pl.BlockSpec(bs, im) for _, bs, im in extras]
    out_specs = [pl.BlockSpec((bm, bn), lambda m, n, k: (m, n))]
    out_shape = [jax.ShapeDtypeStruct((M, N), out_dtype)]
    if side is not None:
        src, sl = side
        n_steps = grid[0] * grid[1] * grid[2]
        R, C = src.shape[1:]
        rows = R // n_steps
        assert R % n_steps == 0 and rows % 16 == 0
        step = lambda m, n, k: (m * grid[1] + n) * grid[2] + k
        args.append(src)
        in_specs.append(pl.BlockSpec((None, rows, C), lambda m, n, k: (sl, step(m, n, k), 0)))
        out_specs.append(pl.BlockSpec((rows, C), lambda m, n, k: (step(m, n, k), 0)))
        out_shape.append(jax.ShapeDtypeStruct((R, C), BF16))
    kern = functools.partial(_mm_kernel, epilogue=epilogue, n_extra=len(extras), n_k=n_k,
                             sub_rows=sub_rows, has_side=side is not None)
    out = pl.pallas_call(
        kern, grid=grid, in_specs=in_specs, out_specs=out_specs, out_shape=out_shape,
        compiler_params=_params(("parallel", "parallel", "arbitrary")),
        name="matmul")(*args)
    return out if side is not None else out[0]


def _retention_tables(chunk, n_seq=1):
    logg = jnp.log1p(-jnp.exp2(-5.0 - jnp.arange(N_HEADS, dtype=F32)))
    idx = jnp.arange(chunk, dtype=F32)
    diff = idx[:, None] - idx[None, :]
    causal = diff >= 0
    dmask = jnp.where(causal[None], jnp.exp(jnp.where(causal, diff, 0.0)[None] * logg[:, None, None]), 0.0)
    xi = jnp.exp((idx + 1.0)[None, :] * logg[:, None])
    zeta = jnp.exp((chunk - 1.0 - idx)[None, :] * logg[:, None])
    g_chunk = jnp.exp(chunk * logg)
    eye = jnp.eye(n_seq, dtype=F32)
    dmask = (eye[None, :, None, :, None] * dmask[:, None, :, None, :]).reshape(
        N_HEADS, n_seq * chunk, n_seq * chunk)
    xi = jnp.tile(xi, (1, n_seq))[:, :, None]
    zeta = jnp.tile(zeta, (1, n_seq))[:, :, None]
    return dmask, xi, zeta, g_chunk[:, None, None]


def _head_norm_gate(o, gs):
    return o * lax.rsqrt(jnp.mean(o * o, axis=-1, keepdims=True) + EPS) * gs


def _retention_chunk(q, kf, v, gs, R, dmask, xi, zeta, g_chunk):
    s = lax.dot_general(q, kf.astype(BF16), (((1,), (1,)), ((), ())), preferred_element_type=F32)
    s = s * dmask
    inner = jnp.dot(s.astype(BF16), v, preferred_element_type=F32)
    cross = jnp.dot(q, R.astype(BF16), preferred_element_type=F32) * xi
    o = inner + cross
    kz = (kf * zeta).astype(BF16)
    upd = lax.dot_general(kz, v, (((0,), (0,)), ((), ())), preferred_element_type=F32)
    r_new = g_chunk * R + upd
    return _head_norm_gate(o, gs), r_new


def _alias_previous(prev, args, in_specs, out_index):
    if prev is None:
        return {}
    args.append(prev)
    in_specs.append(pl.BlockSpec(memory_space=pl.ANY))
    return {len(args) - 1: out_index}


def _ret_prompt_kernel(q_ref, k_ref, v_ref, gs_ref, dm_ref, xi_ref, zt_ref, gc_ref, *rest, chunk):
    o_ref, st_ref = rest[-2:]
    st_ref[...] = jnp.zeros_like(st_ref)
    for c in range(q_ref.shape[0] // chunk):
        rows = pl.ds(c * chunk, chunk)
        og, r_new = _retention_chunk(q_ref[rows, :], k_ref[rows, :], v_ref[rows, :], gs_ref[rows, :],
                                     st_ref[...], dm_ref[...], xi_ref[...], zt_ref[...], gc_ref[...])
        o_ref[rows, :] = og.astype(o_ref.dtype)
        st_ref[...] = r_new


def retention_prompt(q, k, v, gs, B, S, layer, n_layers, prev_states):
    M, D = q.shape
    DV = v.shape[1]
    dk, dv = D // N_HEADS, DV // N_HEADS
    C = CHUNK if S % CHUNK == 0 else S
    dmask, xi, zeta, gch = _retention_tables(C)
    tok = lambda width: pl.BlockSpec((S, width), lambda b, h: (b, h))
    tab = lambda shape: pl.BlockSpec((None,) + shape, lambda b, h: (h, 0, 0))
    args = [q, k, v, gs, dmask, xi, zeta, gch]
    in_specs = [tok(dk), tok(dk), tok(dv), tok(dv), tab((C, C)), tab((C, 1)), tab((C, 1)), tab((1, 1))]
    aliases = _alias_previous(prev_states, args, in_specs, 1)
    return pl.pallas_call(
        functools.partial(_ret_prompt_kernel, chunk=C), grid=(B, N_HEADS),
        in_specs=in_specs,
        out_specs=[tok(dv), pl.BlockSpec((None, None, None, dk, dv), lambda b, h: (layer, b, h, 0, 0))],
        out_shape=(jax.ShapeDtypeStruct((M, DV), BF16),
                   jax.ShapeDtypeStruct((n_layers, B, N_HEADS, dk, dv), F32)),
        input_output_aliases=aliases,
        compiler_params=_params(("parallel", "parallel")),
        name="retention_prompt")(*args)


def _ret_sample_kernel(q_ref, k_ref, v_ref, gs_ref, st_ref, dm_ref, xi_ref, zt_ref, gc_ref, *rest, seq_rows):
    o_ref, so_ref = rest[-2:]
    n_seq, n_heads, dk, dv = st_ref.shape
    seq_of_row = lax.broadcasted_iota(jnp.int32, (q_ref.shape[0], 1), 0) // seq_rows
    for h in range(n_heads):
        ck = pl.ds(h * dk, dk)
        cv = pl.ds(h * dv, dv)
        q, kf, v = q_ref[:, ck], k_ref[:, ck], v_ref[:, cv]
        s = lax.dot_general(q, kf.astype(BF16), (((1,), (1,)), ((), ())), preferred_element_type=F32)
        o = jnp.dot((s * dm_ref[h]).astype(BF16), v, preferred_element_type=F32)
        kz = kf * zt_ref[h]
        for b in range(n_seq):
            mine = seq_of_row == b
            R = st_ref[b, h]
            cross = jnp.dot(q, R.astype(BF16), preferred_element_type=F32) * xi_ref[h]
            o = o + jnp.where(mine, cross, 0.0)
            upd = lax.dot_general(jnp.where(mine, kz, 0.0).astype(BF16), v, (((0,), (0,)), ((), ())),
                                  preferred_element_type=F32)
            so_ref[b, h] = gc_ref[h] * R + upd
        o_ref[:, cv] = _head_norm_gate(o, gs_ref[:, cv]).astype(o_ref.dtype)


def retention_sample(q, k, v, gs, o_buf, states, layer, prev_states, row0, T):
    M, D = q.shape
    DV = v.shape[1]
    _, BS, H, dk, dv = states.shape
    tile = BF16_TILE_ROWS
    n_seq = tile // T
    hg = _pick_block(H, 4, 1)
    assert tile % T == 0 and BS % n_seq == 0 and row0 % tile == 0
    t0 = row0 // tile
    tok = lambda width: pl.BlockSpec((tile, hg * width), lambda i, g: (t0 + i, g))
    st = pl.BlockSpec((None, n_seq, hg, dk, dv), lambda i, g: (layer, i, g, 0, 0))
    tab = lambda shape: pl.BlockSpec((hg,) + shape, lambda i, g: (g, 0, 0))
    dmask, xi, zeta, gch = _retention_tables(T, n_seq)
    args = [q, k, v, gs, states, dmask, xi, zeta, gch, o_buf]
    in_specs = [tok(dk), tok(dk), tok(dv), tok(dv), st, tab((tile, tile)), tab((tile, 1)), tab((tile, 1)),
                tab((1, 1)), pl.BlockSpec(memory_space=pl.ANY)]
    aliases = {len(args) - 1: 0}
    aliases.update(_alias_previous(prev_states, args, in_specs, 1))
    return pl.pallas_call(
        functools.partial(_ret_sample_kernel, seq_rows=T), grid=(BS // n_seq, H // hg),
        in_specs=in_specs,
        out_specs=[tok(dv), st],
        out_shape=(jax.ShapeDtypeStruct((M, DV), BF16), jax.ShapeDtypeStruct(states.shape, F32)),
        input_output_aliases=aliases,
        compiler_params=_params(("parallel", "parallel")),
        name="retention_sample")(*args)


def _pool_prompt_kernel(u_ref, w_ref, sc_ref, o_ref, hist_ref, *, hist_rows):
    g = pl.program_id(0)
    t = pl.program_id(2)
    tb = u_ref.shape[0]

    @pl.when(t == 0)
    def _():
        hist_ref[pl.ds(0, hist_rows), :] = jnp.zeros((hist_rows, hist_ref.shape[1]), F32)

    u = u_ref[...]
    hist_ref[pl.ds(hist_rows, tb), :] = u
    row = (t * tb + lax.broadcasted_iota(jnp.int32, (tb, 1), 0) + 1).astype(F32)
    for gi, win in enumerate(POOL_WINDOWS):

        @pl.when(g == gi)
        def _(win=win):
            ws = u
            for j in range(1, win):
                ws = ws + hist_ref[pl.ds(hist_rows - j, tb), :]
            cnt = jnp.minimum(row, float(win))
            d = ws / cnt - u
            y = jnp.dot(d.astype(BF16), w_ref[...].astype(BF16), preferred_element_type=F32)
            o_ref[...] = y * sc_ref[...]

    hist_ref[pl.ds(0, hist_rows), :] = hist_ref[pl.ds(tb, hist_rows), :]


def pool_prompt(u, w_pool, scale, layer, B, S):
    M, D = u.shape
    G = len(POOL_WINDOWS)
    pg = D // G
    tb = _pick_block(S, 512, 8)
    nt = S // tb
    hist_rows = 16
    assert hist_rows >= max(POOL_WINDOWS) - 1 and tb >= hist_rows
    kern = functools.partial(_pool_prompt_kernel, hist_rows=hist_rows)
    blk = pl.BlockSpec((tb, pg), lambda g, b, t: (b * nt + t, g))
    return pl.pallas_call(
        kern, grid=(G, B, nt),
        in_specs=[blk, pl.BlockSpec((None, None, pg, pg), lambda g, b, t: (layer, g, 0, 0)),
                  pl.BlockSpec((None, 1, pg), lambda g, b, t: (layer, 0, g))],
        out_specs=blk,
        out_shape=jax.ShapeDtypeStruct((M, D), F32),
        scratch_shapes=[pltpu.VMEM((hist_rows + tb, pg), F32)],
        compiler_params=_params(("arbitrary", "arbitrary", "arbitrary")),
        name="pool_prompt")(u, w_pool, scale)


def _pool_sample_kernel(buf_ref, u_ref, w_ref, sc_ref, o_ref, *, pos0):
    g = pl.program_id(0)
    nbuf = buf_ref.shape[0]
    T = u_ref.shape[0]
    ext = lambda j: buf_ref[j] if j < nbuf else u_ref[j - nbuf]
    for gi, win in enumerate(POOL_WINDOWS):

        @pl.when(g == gi)
        def _(win=win):
            for t in range(T):
                ws = ext(nbuf + t)
                for j in range(1, win):
                    ws = ws + ext(nbuf + t - j)
                cnt = float(min(win, pos0 + t + 1))
                d = ws / cnt - u_ref[t]
                y = jnp.dot(d.astype(BF16), w_ref[...].astype(BF16), preferred_element_type=F32)
                o_ref[t] = y * sc_ref[...]


def pool_sample(buf_t, u_t, w_pool, scale, layer, pos0):
    nbuf, BS, D = buf_t.shape
    T = u_t.shape[0]
    G = len(POOL_WINDOWS)
    pg = D // G
    assert nbuf >= max(POOL_WINDOWS) - 1
    kern = functools.partial(_pool_sample_kernel, pos0=pos0)
    return pl.pallas_call(
        kern, grid=(G,),
        in_specs=[pl.BlockSpec((nbuf, BS, pg), lambda g: (0, 0, g)),
                  pl.BlockSpec((T, BS, pg), lambda g: (0, 0, g)),
                  pl.BlockSpec((None, None, pg, pg), lambda g: (layer, g, 0, 0)),
                  pl.BlockSpec((None, 1, pg), lambda g: (layer, 0, g))],
        out_specs=pl.BlockSpec((T, BS, pg), lambda g: (0, 0, g)),
        out_shape=jax.ShapeDtypeStruct((T, BS, D), F32),
        compiler_params=_params(("parallel",)),
        name="pool_sample")(buf_t, u_t, w_pool, scale)


def _rotary_tables(pos, half):
    inv = ROPE_BASE ** (-jnp.arange(half, dtype=F32) / half)
    ang = pos.astype(F32)[:, None] * inv[None, :]
    return jnp.cos(ang), jnp.sin(ang)


def kernel(x_prompt, x_sample, state_ret, state_pool, w_ret_in, w_ret_out, w_pool, pool_scale,
           w_up, w_down, g_mix_pre, g_mix_post, g_mlp_pre, g_mlp_post):
    B, S, D = x_prompt.shape
    BS, TS, _ = x_sample.shape
    BP, MS = B * S, BS * TS
    M = BP + MS
    depth = w_up.shape[0]
    dk = D // N_HEADS
    DV = w_ret_out.shape[1]
    nbuf = state_pool.shape[2]
    n_ret = state_ret.shape[0]
    assert dk // 2 == LANES and TS <= nbuf and TS % CHUNK != 0

    pos = jnp.concatenate([jnp.tile(jnp.arange(S), B), jnp.tile(PAST_LEN + jnp.arange(TS), BS)])
    cos, sin = _rotary_tables(pos, dk // 2)
    bm = _pick_block(M, 1088, 16)
    rot_extras = tuple((t, (bm, dk // 2), lambda m, n, k: (m, 0)) for t in (cos, sin))
    scale3 = pool_scale.reshape(pool_scale.shape[0], 1, D)

    x = (x_prompt.reshape(BP, D), x_sample.reshape(MS, D))
    ret_p = ret_s = None
    pool_p, pool_s = [], []
    u = prenorm(*x, g_mix_pre[0], BF16)
    for i in range(depth):
        j = i // 2
        if i % 2 == 0:
            q = matmul(u, w_ret_in, j, col0=0, n_cols=D, out_dtype=BF16, extras=rot_extras,
                       epilogue=functools.partial(_epi_rotary, head_dk=dk, scale=1.0))
            k = matmul(u, w_ret_in, j, col0=D, n_cols=D, out_dtype=F32, extras=rot_extras,
                       epilogue=functools.partial(_epi_rotary, head_dk=dk, scale=dk ** -0.5))
            v, w_out_b = matmul(u, w_ret_in, j, col0=2 * D, n_cols=DV, out_dtype=BF16, side=(w_ret_out, j))
            gs = matmul(u, w_ret_in, j, col0=2 * D + DV, n_cols=DV, out_dtype=F32, epilogue=_epi_silu)
            o, ret_p = retention_prompt(q, k, v, gs, B, S, j, n_ret, ret_p)
            o, ret_s = retention_sample(q, k, v, gs, o, state_ret, j, ret_s, BP, TS)
            m = matmul(o, w_out_b, bk=4096)
        else:
            u_s = u[BP:].reshape(BS, TS, D)
            m = pool_prompt(u, w_pool, scale3, j, B, S)
            y_s = pool_sample(state_pool[j].transpose(1, 0, 2), u_s.transpose(1, 0, 2), w_pool,
                              scale3, j, PAST_LEN)
            m = lax.dynamic_update_slice(m, y_s.transpose(1, 0, 2).reshape(MS, D), (BP, 0))
            pool_p.append(jnp.stack([u[(b + 1) * S - nbuf:(b + 1) * S] for b in range(B)]))
            pool_s.append(jnp.concatenate([state_pool[j][:, TS:], u_s], axis=1))
        x, un = resid_norm(x, m, g_mix_post[i], g_mlp_pre[i], BF16, BP)
        h, w_down_b = matmul(un, w_up, i, out_dtype=BF16, epilogue=_epi_relu2, side=(w_down, i))
        hd = matmul(h, w_down_b, bk=4096)
        if i + 1 < depth:
            x, u = resid_norm(x, hd, g_mlp_post[i], g_mix_pre[i + 1], BF16 if (i + 1) % 2 == 0 else F32, BP)
        else:
            y_p, y_s = resid_split(x, hd, g_mlp_post[i], BP)

    return (y_p.reshape(B, S, D), y_s.reshape(BS, TS, D),
            ret_p, jnp.stack(pool_p), ret_s, jnp.stack(pool_s))
```

```python
import functools
import math

import jax
import jax.numpy as jnp
from jax import lax
from jax.experimental import pallas as pl
from jax.experimental.pallas import tpu as pltpu

N_HEADS = 16
CHUNK = 128
ROPE_BASE = 10000.0
POOL_WINDOWS = (2, 4, 8, 16)
PAST_LEN = 16384
EPS = 1e-6

F32 = jnp.float32
BF16 = jnp.bfloat16

VMEM_BYTES_V7X = 64 * 1024 * 1024
VMEM_LIMIT_BYTES = VMEM_BYTES_V7X - 8 * 1024 * 1024
LANES = 128
BF16_TILE_ROWS = 16


def _pick_block(n, target, mult):
    best = None
    for d in range(mult, min(n, target) + 1, mult):
        if n % d == 0:
            best = d
    return best if best is not None else n


def _params(sem):
    return pltpu.CompilerParams(dimension_semantics=sem, vmem_limit_bytes=VMEM_LIMIT_BYTES)


def _rms(x, g):
    ms = jnp.mean(x * x, axis=-1, keepdims=True)
    return x * lax.rsqrt(ms + EPS) * g


def _stacked_block(xp_ref, xs_ref, nbp):
    return jnp.where(pl.program_id(0) < nbp, xp_ref[...], xs_ref[...])


def _prenorm_kernel(xp_ref, xs_ref, g_ref, u_ref, *, nbp):
    u_ref[...] = _rms(_stacked_block(xp_ref, xs_ref, nbp), g_ref[...]).astype(u_ref.dtype)


def _resid_norm_kernel(*refs, nbp, n_x, n_m):
    x_refs, m_refs = refs[:n_x], refs[n_x:n_x + n_m]
    gpost_ref, gpre_ref, xo_ref, u_ref = refs[n_x + n_m:]
    x = x_refs[0][...] if n_x == 1 else _stacked_block(*x_refs, nbp)
    m = m_refs[0][...] if n_m == 1 else _stacked_block(*m_refs, nbp)
    xn = x + _rms(m, gpost_ref[...])
    xo_ref[...] = xn
    u_ref[...] = _rms(xn, gpre_ref[...]).astype(u_ref.dtype)


def _resid_split_kernel(x_ref, m_ref, gpost_ref, yp_ref, ys_ref, *, nbp):
    y = x_ref[...] + _rms(m_ref[...], gpost_ref[...])
    i = pl.program_id(0)

    @pl.when(i < nbp)
    def _():
        yp_ref[...] = y

    @pl.when(i >= nbp)
    def _():
        ys_ref[...] = y


def _row_specs(BP, MS, D):
    br = _pick_block(math.gcd(BP, MS), 256, 16)
    nbp, nbs = BP // br, MS // br
    whole = pl.BlockSpec((br, D), lambda i: (i, 0))
    prompt = pl.BlockSpec((br, D), lambda i: (jnp.minimum(i, nbp - 1), 0))
    sample = pl.BlockSpec((br, D), lambda i: (jnp.maximum(i - nbp, 0), 0))
    vec = pl.BlockSpec((1, D), lambda i: (0, 0))
    return nbp, (nbp + nbs,), whole, prompt, sample, vec


def prenorm(xp, xs, g, dtype):
    (BP, D), MS = xp.shape, xs.shape[0]
    nbp, grid, whole, prompt, sample, vec = _row_specs(BP, MS, D)
    return pl.pallas_call(
        functools.partial(_prenorm_kernel, nbp=nbp), grid=grid,
        in_specs=[prompt, sample, vec], out_specs=whole,
        out_shape=jax.ShapeDtypeStruct((BP + MS, D), dtype), compiler_params=_params(("parallel",)),
        name="prenorm")(xp, xs, g.reshape(1, D))


def resid_norm(x, m, g_post, g_pre, dtype, BP, MS):
    D = g_post.shape[0]
    nbp, grid, whole, prompt, sample, vec = _row_specs(BP, MS, D)
    as_args = lambda a: (a, [prompt, sample]) if isinstance(a, tuple) else ((a,), [whole])
    (xs, x_specs), (ms, m_specs) = as_args(x), as_args(m)
    return pl.pallas_call(
        functools.partial(_resid_norm_kernel, nbp=nbp, n_x=len(xs), n_m=len(ms)), grid=grid,
        in_specs=x_specs + m_specs + [vec, vec], out_specs=[whole, whole],
        out_shape=(jax.ShapeDtypeStruct((BP + MS, D), F32), jax.ShapeDtypeStruct((BP + MS, D), dtype)),
        compiler_params=_params(("parallel",)), name="resid_norm")(
            *xs, *ms, g_post.reshape(1, D), g_pre.reshape(1, D))


def resid_split(x, m, g_post, BP):
    M, D = x.shape
    nbp, grid, whole, prompt, sample, vec = _row_specs(BP, M - BP, D)
    return pl.pallas_call(
        functools.partial(_resid_split_kernel, nbp=nbp), grid=grid,
        in_specs=[whole, whole, vec], out_specs=[prompt, sample],
        out_shape=(jax.ShapeDtypeStruct((BP, D), F32), jax.ShapeDtypeStruct((M - BP, D), F32)),
        compiler_params=_params(("arbitrary",)), name="resid_split")(x, m, g_post.reshape(1, D))


def _epi_plain(acc, rows, o_ref):
    o_ref[rows, :] = acc.astype(o_ref.dtype)


def _epi_relu2(acc, rows, o_ref):
    r = jnp.maximum(acc, 0.0)
    o_ref[rows, :] = (r * r).astype(o_ref.dtype)


def _epi_silu(acc, rows, o_ref):
    o_ref[rows, :] = (acc * (1.0 / (1.0 + jnp.exp(-acc)))).astype(o_ref.dtype)


def _epi_rotary(acc, rows, o_ref, cos_ref, sin_ref, *, head_dk, scale):
    half = head_dk // 2
    cos = cos_ref[rows, :]
    sin = sin_ref[rows, :]
    for h in range(acc.shape[1] // head_dk):
        c0 = h * head_dk
        x1 = acc[:, c0:c0 + half]
        x2 = acc[:, c0 + half:c0 + head_dk]
        o_ref[rows, c0:c0 + half] = ((x1 * cos - x2 * sin) * scale).astype(o_ref.dtype)
        o_ref[rows, c0 + half:c0 + head_dk] = ((x2 * cos + x1 * sin) * scale).astype(o_ref.dtype)


def _mm_kernel(*refs, epilogue, n_extra, n_k, sub_rows, has_side, nbp):
    if nbp is None:
        x_ref, w_ref, *rest = refs
        lhs = lambda rows: x_ref[rows, :]
    else:
        x_ref, xs_ref, w_ref, *rest = refs
        lhs = lambda rows: jnp.where(pl.program_id(0) < nbp, x_ref[rows, :], xs_ref[rows, :])
    extra = rest[:n_extra]
    if has_side:
        side_in_ref, o_ref, side_out_ref = rest[n_extra:n_extra + 3]
        side_out_ref[...] = side_in_ref[...].astype(BF16)
    else:
        o_ref = rest[n_extra]
    if w_ref.dtype == BF16:
        weights = lambda: w_ref[...]
    else:
        w_bf16 = w_ref[...].astype(BF16)
        weights = lambda: w_bf16
    bm = x_ref.shape[0]
    row_slices = [pl.ds(r, sub_rows) for r in range(0, bm, sub_rows)]
    if n_k == 1:
        for rows in row_slices:
            acc = jnp.dot(lhs(rows), weights(), preferred_element_type=F32)
            epilogue(acc, rows, o_ref, *extra)
        return

    @pl.when(pl.program_id(2) == 0)
    def _():
        for rows in row_slices:
            o_ref[rows, :] = jnp.dot(lhs(rows), weights(), preferred_element_type=F32)

    @pl.when(pl.program_id(2) > 0)
    def _():
        for rows in row_slices:
            o_ref[rows, :] += jnp.dot(lhs(rows), weights(), preferred_element_type=F32)


def matmul(x, w, layer=None, *, col0=0, n_cols=None, out_dtype=F32, epilogue=_epi_plain, extras=(),
           bk=None, side=None):
    N = n_cols if n_cols is not None else w.shape[-1]
    if isinstance(x, tuple):
        (BP, K), MS = x[0].shape, x[1].shape[0]
        M = BP + MS
        bm = _pick_block(math.gcd(BP, MS), 1088, 16)
        nbp = BP // bm
    else:
        M, K = x.shape
        bm = _pick_block(M, 1088, 16)
        nbp = None
    bn = _pick_block(N, 1024 if w.dtype == BF16 else 512, 256)
    bk = K if bk is None else min(bk, K)
    n_k = K // bk
    sub_rows = _pick_block(bm, 272, 16)
    cb = col0 // bn
    grid = (M // bm, N // bn, n_k)
    assert col0 % bn == 0 and K % bk == 0
    assert n_k == 1 or (epilogue is _epi_plain and out_dtype == F32)
    if w.ndim == 2:
        w_spec = pl.BlockSpec((bk, bn), lambda m, n, k: (k, n + cb))
    else:
        w_spec = pl.BlockSpec((None, bk, bn), lambda m, n, k: (layer, k, n + cb))
    if nbp is None:
        x_args = [x]
        x_specs = [pl.BlockSpec((bm, bk), lambda m, n, k: (m, k))]
    else:
        x_args = list(x)
        x_specs = [pl.BlockSpec((bm, bk), lambda m, n, k: (jnp.minimum(m, nbp - 1), k)),
                   pl.BlockSpec((bm, bk), lambda m, n, k: (jnp.maximum(m - nbp, 0), k))]
    args = x_args + [w] + [a for a, _, _ in extras]
    in_specs = x_specs + [w_spec]
    in_specs += [pl.BlockSpec(bs, im) for _, bs, im in extras]
    out_specs = [pl.BlockSpec((bm, bn), lambda m, n, k: (m, n))]
    out_shape = [jax.ShapeDtypeStruct((M, N), out_dtype)]
    if side is not None:
        src, sl = side
        n_steps = grid[0] * grid[1] * grid[2]
        R, C = src.shape[1:]
        rows = R // n_steps
        assert R % n_steps == 0 and rows % 16 == 0
        step = lambda m, n, k: (m * grid[1] + n) * grid[2] + k
        args.append(src)
        in_specs.append(pl.BlockSpec((None, rows, C), lambda m, n, k: (sl, step(m, n, k), 0)))
        out_specs.append(pl.BlockSpec((rows, C), lambda m, n, k: (step(m, n, k), 0)))
        out_shape.append(jax.ShapeDtypeStruct((R, C), BF16))
    kern = functools.partial(_mm_kernel, epilogue=epilogue, n_extra=len(extras), n_k=n_k,
                             sub_rows=sub_rows, has_side=side is not None, nbp=nbp)
    out = pl.pallas_call(
        kern, grid=grid, in_specs=in_specs, out_specs=out_specs, out_shape=out_shape,
        compiler_params=_params(("parallel", "parallel", "arbitrary")),
        name="matmul")(*args)
    return out if side is not None else out[0]


def _retention_tables(chunk, n_seq=1):
    logg = jnp.log1p(-jnp.exp2(-5.0 - jnp.arange(N_HEADS, dtype=F32)))
    idx = jnp.arange(chunk, dtype=F32)
    diff = idx[:, None] - idx[None, :]
    causal = diff >= 0
    dmask = jnp.where(causal[None], jnp.exp(jnp.where(causal, diff, 0.0)[None] * logg[:, None, None]), 0.0)
    xi = jnp.exp((idx + 1.0)[None, :] * logg[:, None])
    zeta = jnp.exp((chunk - 1.0 - idx)[None, :] * logg[:, None])
    g_chunk = jnp.exp(chunk * logg)
    eye = jnp.eye(n_seq, dtype=F32)
    dmask = (eye[None, :, None, :, None] * dmask[:, None, :, None, :]).reshape(
        N_HEADS, n_seq * chunk, n_seq * chunk)
    xi = jnp.tile(xi, (1, n_seq))[:, :, None]
    zeta = jnp.tile(zeta, (1, n_seq))[:, :, None]
    return dmask, xi, zeta, g_chunk[:, None, None]


def _head_norm_gate(o, gs):
    return o * lax.rsqrt(jnp.mean(o * o, axis=-1, keepdims=True) + EPS) * gs


def _retention_chunk(q, kf, v, gs, R, dmask, xi, zeta, g_chunk):
    s = lax.dot_general(q, kf.astype(BF16), (((1,), (1,)), ((), ())), preferred_element_type=F32)
    s = s * dmask
    inner = jnp.dot(s.astype(BF16), v, preferred_element_type=F32)
    cross = jnp.dot(q, R.astype(BF16), preferred_element_type=F32) * xi
    o = inner + cross
    kz = (kf * zeta).astype(BF16)
    upd = lax.dot_general(kz, v, (((0,), (0,)), ((), ())), preferred_element_type=F32)
    r_new = g_chunk * R + upd
    return _head_norm_gate(o, gs), r_new


def _alias_previous(prev, args, in_specs, out_index):
    if prev is None:
        return {}
    args.append(prev)
    in_specs.append(pl.BlockSpec(memory_space=pl.ANY))
    return {len(args) - 1: out_index}


def _ret_prompt_kernel(q_ref, k_ref, v_ref, gs_ref, dm_ref, xi_ref, zt_ref, gc_ref, *rest, chunk):
    o_ref, st_ref = rest[-2:]
    st_ref[...] = jnp.zeros_like(st_ref)
    for c in range(q_ref.shape[0] // chunk):
        rows = pl.ds(c * chunk, chunk)
        og, r_new = _retention_chunk(q_ref[rows, :], k_ref[rows, :], v_ref[rows, :], gs_ref[rows, :],
                                     st_ref[...], dm_ref[...], xi_ref[...], zt_ref[...], gc_ref[...])
        o_ref[rows, :] = og.astype(o_ref.dtype)
        st_ref[...] = r_new


def retention_prompt(q, k, v, gs, B, S, layer, n_layers, prev_states):
    M, D = q.shape
    DV = v.shape[1]
    dk, dv = D // N_HEADS, DV // N_HEADS
    C = CHUNK if S % CHUNK == 0 else S
    dmask, xi, zeta, gch = _retention_tables(C)
    tok = lambda width: pl.BlockSpec((S, width), lambda b, h: (b, h))
    tab = lambda shape: pl.BlockSpec((None,) + shape, lambda b, h: (h, 0, 0))
    args = [q, k, v, gs, dmask, xi, zeta, gch]
    in_specs = [tok(dk), tok(dk), tok(dv), tok(dv), tab((C, C)), tab((C, 1)), tab((C, 1)), tab((1, 1))]
    aliases = _alias_previous(prev_states, args, in_specs, 1)
    return pl.pallas_call(
        functools.partial(_ret_prompt_kernel, chunk=C), grid=(B, N_HEADS),
        in_specs=in_specs,
        out_specs=[tok(dv), pl.BlockSpec((None, None, None, dk, dv), lambda b, h: (layer, b, h, 0, 0))],
        out_shape=(jax.ShapeDtypeStruct((B * S, DV), BF16),
                   jax.ShapeDtypeStruct((n_layers, B, N_HEADS, dk, dv), F32)),
        input_output_aliases=aliases,
        compiler_params=_params(("parallel", "parallel")),
        name="retention_prompt")(*args)


def _ret_sample_kernel(q_ref, k_ref, v_ref, gs_ref, st_ref, dm_ref, xi_ref, zt_ref, gc_ref, *rest, seq_rows):
    o_ref, so_ref = rest[-2:]
    n_seq, n_heads, dk, dv = st_ref.shape
    seq_of_row = lax.broadcasted_iota(jnp.int32, (q_ref.shape[0], 1), 0) // seq_rows
    for h in range(n_heads):
        ck = pl.ds(h * dk, dk)
        cv = pl.ds(h * dv, dv)
        q, kf, v = q_ref[:, ck], k_ref[:, ck], v_ref[:, cv]
        s = lax.dot_general(q, kf.astype(BF16), (((1,), (1,)), ((), ())), preferred_element_type=F32)
        o = jnp.dot((s * dm_ref[h]).astype(BF16), v, preferred_element_type=F32)
        kz = kf * zt_ref[h]
        for b in range(n_seq):
            mine = seq_of_row == b
            R = st_ref[b, h]
            cross = jnp.dot(q, R.astype(BF16), preferred_element_type=F32) * xi_ref[h]
            o = o + jnp.where(mine, cross, 0.0)
            upd = lax.dot_general(jnp.where(mine, kz, 0.0).astype(BF16), v, (((0,), (0,)), ((), ())),
                                  preferred_element_type=F32)
            so_ref[b, h] = gc_ref[h] * R + upd
        o_ref[:, cv] = _head_norm_gate(o, gs_ref[:, cv]).astype(o_ref.dtype)


def retention_sample(q, k, v, gs, states, layer, prev_states, row0, T):
    M, D = q.shape
    DV = v.shape[1]
    _, BS, H, dk, dv = states.shape
    tile = BF16_TILE_ROWS
    n_seq = tile // T
    hg = _pick_block(H, 4, 1)
    assert tile % T == 0 and BS % n_seq == 0 and row0 % tile == 0
    t0 = row0 // tile
    tok = lambda width: pl.BlockSpec((tile, hg * width), lambda i, g: (t0 + i, g))
    st = pl.BlockSpec((None, n_seq, hg, dk, dv), lambda i, g: (layer, i, g, 0, 0))
    tab = lambda shape: pl.BlockSpec((hg,) + shape, lambda i, g: (g, 0, 0))
    dmask, xi, zeta, gch = _retention_tables(T, n_seq)
    args = [q, k, v, gs, states, dmask, xi, zeta, gch]
    in_specs = [tok(dk), tok(dk), tok(dv), tok(dv), st, tab((tile, tile)), tab((tile, 1)), tab((tile, 1)),
                tab((1, 1))]
    aliases = _alias_previous(prev_states, args, in_specs, 1)
    return pl.pallas_call(
        functools.partial(_ret_sample_kernel, seq_rows=T), grid=(BS // n_seq, H // hg),
        in_specs=in_specs,
        out_specs=[pl.BlockSpec((tile, hg * dv), lambda i, g: (i, g)), st],
        out_shape=(jax.ShapeDtypeStruct((BS * T, DV), BF16), jax.ShapeDtypeStruct(states.shape, F32)),
        input_output_aliases=aliases,
        compiler_params=_params(("parallel", "parallel")),
        name="retention_sample")(*args)


def _pool_prompt_kernel(u_ref, w_ref, sc_ref, o_ref, hist_ref, *, hist_rows):
    g = pl.program_id(0)
    t = pl.program_id(2)
    tb = u_ref.shape[0]

    @pl.when(t == 0)
    def _():
        hist_ref[pl.ds(0, hist_rows), :] = jnp.zeros((hist_rows, hist_ref.shape[1]), F32)

    u = u_ref[...]
    hist_ref[pl.ds(hist_rows, tb), :] = u
    row = (t * tb + lax.broadcasted_iota(jnp.int32, (tb, 1), 0) + 1).astype(F32)
    for gi, win in enumerate(POOL_WINDOWS):

        @pl.when(g == gi)
        def _(win=win):
            ws = u
            for j in range(1, win):
                ws = ws + hist_ref[pl.ds(hist_rows - j, tb), :]
            cnt = jnp.minimum(row, float(win))
            d = ws / cnt - u
            y = jnp.dot(d.astype(BF16), w_ref[...].astype(BF16), preferred_element_type=F32)
            o_ref[...] = y * sc_ref[...]

    hist_ref[pl.ds(0, hist_rows), :] = hist_ref[pl.ds(tb, hist_rows), :]


def pool_prompt(u, w_pool, scale, layer, B, S):
    M, D = u.shape
    G = len(POOL_WINDOWS)
    pg = D // G
    tb = _pick_block(S, 512, 8)
    nt = S // tb
    hist_rows = 16
    assert hist_rows >= max(POOL_WINDOWS) - 1 and tb >= hist_rows
    kern = functools.partial(_pool_prompt_kernel, hist_rows=hist_rows)
    blk = pl.BlockSpec((tb, pg), lambda g, b, t: (b * nt + t, g))
    return pl.pallas_call(
        kern, grid=(G, B, nt),
        in_specs=[blk, pl.BlockSpec((None, None, pg, pg), lambda g, b, t: (layer, g, 0, 0)),
                  pl.BlockSpec((None, 1, pg), lambda g, b, t: (layer, 0, g))],
        out_specs=blk,
        out_shape=jax.ShapeDtypeStruct((B * S, D), F32),
        scratch_shapes=[pltpu.VMEM((hist_rows + tb, pg), F32)],
        compiler_params=_params(("arbitrary", "arbitrary", "arbitrary")),
        name="pool_prompt")(u, w_pool, scale)


def _pool_sample_kernel(buf_ref, u_ref, w_ref, sc_ref, o_ref, *, pos0):
    g = pl.program_id(0)
    nbuf = buf_ref.shape[0]
    T = u_ref.shape[0]
    ext = lambda j: buf_ref[j] if j < nbuf else u_ref[j - nbuf]
    for gi, win in enumerate(POOL_WINDOWS):

        @pl.when(g == gi)
        def _(win=win):
            for t in range(T):
                ws = ext(nbuf + t)
                for j in range(1, win):
                    ws = ws + ext(nbuf + t - j)
                cnt = float(min(win, pos0 + t + 1))
                d = ws / cnt - u_ref[t]
                y = jnp.dot(d.astype(BF16), w_ref[...].astype(BF16), preferred_element_type=F32)
                o_ref[t] = y * sc_ref[...]


def pool_sample(buf_t, u_t, w_pool, scale, layer, pos0):
    nbuf, BS, D = buf_t.shape
    T = u_t.shape[0]
    G = len(POOL_WINDOWS)
    pg = D // G
    assert nbuf >= max(POOL_WINDOWS) - 1
    kern = functools.partial(_pool_sample_kernel, pos0=pos0)
    return pl.pallas_call(
        kern, grid=(G,),
        in_specs=[pl.BlockSpec((nbuf, BS, pg), lambda g: (0, 0, g)),
                  pl.BlockSpec((T, BS, pg), lambda g: (0, 0, g)),
                  pl.BlockSpec((None, None, pg, pg), lambda g: (layer, g, 0, 0)),
                  pl.BlockSpec((None, 1, pg), lambda g: (layer, 0, g))],
        out_specs=pl.BlockSpec((T, BS, pg), lambda g: (0, 0, g)),
        out_shape=jax.ShapeDtypeStruct((T, BS, D), F32),
        compiler_params=_params(("parallel",)),
        name="pool_sample")(buf_t, u_t, w_pool, scale)


def _rotary_tables(pos, half):
    inv = ROPE_BASE ** (-jnp.arange(half, dtype=F32) / half)
    ang = pos.astype(F32)[:, None] * inv[None, :]
    return jnp.cos(ang), jnp.sin(ang)


def kernel(x_prompt, x_sample, state_ret, state_pool, w_ret_in, w_ret_out, w_pool, pool_scale,
           w_up, w_down, g_mix_pre, g_mix_post, g_mlp_pre, g_mlp_post):
    B, S, D = x_prompt.shape
    BS, TS, _ = x_sample.shape
    BP, MS = B * S, BS * TS
    M = BP + MS
    depth = w_up.shape[0]
    dk = D // N_HEADS
    DV = w_ret_out.shape[1]
    nbuf = state_pool.shape[2]
    n_ret = state_ret.shape[0]
    assert dk // 2 == LANES and TS <= nbuf and TS % CHUNK != 0

    pos = jnp.concatenate([jnp.tile(jnp.arange(S), B), jnp.tile(PAST_LEN + jnp.arange(TS), BS)])
    cos, sin = _rotary_tables(pos, dk // 2)
    bm = _pick_block(M, 1088, 16)
    rot_extras = tuple((t, (bm, dk // 2), lambda m, n, k: (m, 0)) for t in (cos, sin))
    scale3 = pool_scale.reshape(pool_scale.shape[0], 1, D)

    x = (x_prompt.reshape(BP, D), x_sample.reshape(MS, D))
    ret_p = ret_s = None
    pool_p, pool_s = [], []
    u = prenorm(*x, g_mix_pre[0], BF16)
    for i in range(depth):
        j = i // 2
        if i % 2 == 0:
            q = matmul(u, w_ret_in, j, col0=0, n_cols=D, out_dtype=BF16, extras=rot_extras,
                       epilogue=functools.partial(_epi_rotary, head_dk=dk, scale=1.0))
            k = matmul(u, w_ret_in, j, col0=D, n_cols=D, out_dtype=F32, extras=rot_extras,
                       epilogue=functools.partial(_epi_rotary, head_dk=dk, scale=dk ** -0.5))
            v, w_out_b = matmul(u, w_ret_in, j, col0=2 * D, n_cols=DV, out_dtype=BF16, side=(w_ret_out, j))
            gs = matmul(u, w_ret_in, j, col0=2 * D + DV, n_cols=DV, out_dtype=F32, epilogue=_epi_silu)
            o_p, ret_p = retention_prompt(q, k, v, gs, B, S, j, n_ret, ret_p)
            o_s, ret_s = retention_sample(q, k, v, gs, state_ret, j, ret_s, BP, TS)
            m = matmul((o_p, o_s), w_out_b, bk=4096)
        else:
            u_s = u[BP:].reshape(BS, TS, D)
            y_p = pool_prompt(u, w_pool, scale3, j, B, S)
            y_s = pool_sample(state_pool[j].transpose(1, 0, 2), u_s.transpose(1, 0, 2), w_pool,
                              scale3, j, PAST_LEN)
            m = (y_p, y_s.transpose(1, 0, 2).reshape(MS, D))
            pool_p.append(jnp.stack([u[(b + 1) * S - nbuf:(b + 1) * S] for b in range(B)]))
            pool_s.append(jnp.concatenate([state_pool[j][:, TS:], u_s], axis=1))
        x, un = resid_norm(x, m, g_mix_post[i], g_mlp_pre[i], BF16, BP, MS)
        h, w_down_b = matmul(un, w_up, i, out_dtype=BF16, epilogue=_epi_relu2, side=(w_down, i))
        hd = matmul(h, w_down_b, bk=4096)
        if i + 1 < depth:
            x, u = resid_norm(x, hd, g_mlp_post[i], g_mix_pre[i + 1], BF16 if (i + 1) % 2 == 0 else F32,
                              BP, MS)
        else:
            y_p, y_s = resid_split(x, hd, g_mlp_post[i], BP)

    return (y_p.reshape(B, S, D), y_s.reshape(BS, TS, D),
            ret_p, jnp.stack(pool_p), ret_s, jnp.stack(pool_s))
```
